```python
import jax, jax.numpy as jnp
from jax import lax
import numpy as np

D_MODEL = 1024
BATCH = 4
SEQ = 8192
DEPTH = 2

N_HEADS_A = 8
HEAD_DIM = 64
WIDTH_A = N_HEADS_A * HEAD_DIM
N_IDX_HEADS = 8
IDX_DIM = 32
TOPK_MAX = 256
Q_BLOCK = 128
IDX_SCALE = (N_IDX_HEADS ** -0.5) * (IDX_DIM ** -0.5)
N_GROUPS_B = 8
GROUP_DIM_B = 64
WIDTH_B = N_GROUPS_B * GROUP_DIM_B
CHUNK = 128
IN_SPLITS = (WIDTH_A, HEAD_DIM, HEAD_DIM, N_IDX_HEADS * IDX_DIM, IDX_DIM, N_IDX_HEADS, WIDTH_B, WIDTH_B)
IN_WIDTH = WIDTH_A + 2 * HEAD_DIM + N_IDX_HEADS * IDX_DIM + IDX_DIM + N_IDX_HEADS + 2 * WIDTH_B
CONV_WIDTH = 31
D_FF = 4 * D_MODEL
ROPE_THETA = 10000.0
EPS = 1e-6
N_EVEN = (DEPTH + 1) // 2
N_ODD = DEPTH // 2

kernel_name = "hybrid_dsa_gmlp_conformer_block"


def rms_norm(x, g):
    xf = x.astype(jnp.float32)
    y = xf * lax.rsqrt(jnp.mean(xf * xf, axis=-1, keepdims=True) + EPS)
    return (y * g.astype(jnp.float32)).astype(x.dtype)


def rope(x, positions):
    d = x.shape[-1]
    half = d // 2
    inv_freq = ROPE_THETA ** (-jnp.arange(half, dtype=jnp.float32) * 2.0 / d)
    ang = positions.astype(jnp.float32)[..., None] * inv_freq
    cos = jnp.cos(ang)[:, :, None, :]
    sin = jnp.sin(ang)[:, :, None, :]
    xf = x.astype(jnp.float32)
    x1, x2 = xf[..., :half], xf[..., half:]
    out = jnp.concatenate([x1 * cos - x2 * sin, x2 * cos + x1 * sin], axis=-1)
    return out.astype(x.dtype)


def split_columns(z, sizes):
    outs, start = [], 0
    for s in sizes:
        outs.append(z[..., start:start + s])
        start += s
    return outs


def dsa_attention(q, k, v, qi, ki, wi):
    B, S = q.shape[0], q.shape[1]
    topk = min(TOPK_MAX, S // 4)
    nb = S // Q_BLOCK
    key_pos = jnp.arange(S)
    ki32 = ki.astype(jnp.float32)

    def to_blocks(a):
        return a.reshape((B, nb, Q_BLOCK) + a.shape[2:]).swapaxes(0, 1)

    def block_fn(args):
        blk, qb, qib, wib = args
        t = blk * Q_BLOCK + jnp.arange(Q_BLOCK)
        causal = key_pos[None, :] <= t[:, None]
        logits = jnp.einsum('bqhe,bse->bqhs', qib.astype(jnp.float32), ki32)
        score = jnp.einsum('bqh,bqhs->bqs', wib.astype(jnp.float32), jax.nn.relu(logits))
        score = jnp.where(causal[None], score, -jnp.inf)
        _, idx = lax.top_k(score, topk)
        k_sel = jax.vmap(lambda kk, ii: kk[ii])(k, idx)
        v_sel = jax.vmap(lambda vv, ii: vv[ii])(v, idx)
        valid = idx <= t[None, :, None]
        att = jnp.einsum('bqhd,bqkd->bqhk', qb.astype(jnp.float32),
                         k_sel.astype(jnp.float32)) * (HEAD_DIM ** -0.5)
        att = jnp.where(valid[:, :, None, :], att, -jnp.inf)
        p = jax.nn.softmax(att, axis=-1)
        out = jnp.einsum('bqhk,bqkd->bqhd', p, v_sel.astype(jnp.float32))
        return out.astype(qb.dtype)

    out = lax.map(block_fn, (jnp.arange(nb), to_blocks(q), to_blocks(qi), to_blocks(wi)))
    return out.swapaxes(0, 1).reshape(B, S, -1)


def spatial_gating(uv, g_v, w_s, b_s):
    B, S = uv.shape[0], uv.shape[1]
    u, vv = uv[..., :WIDTH_B], uv[..., WIDTH_B:]
    vv = rms_norm(vv, g_v)
    vv = vv.reshape(B, S // CHUNK, CHUNK, N_GROUPS_B, GROUP_DIM_B)
    w = w_s * jnp.tril(jnp.ones((CHUNK, CHUNK), dtype=w_s.dtype))[None]
    sv = jnp.einsum('gij,bcjgd->bcigd', w, vv) + b_s.T[None, None, :, :, None]
    return u * sv.reshape(B, S, WIDTH_B)


def even_layer(x, positions, g_norm, w_in, g_q, g_k, g_kidx, g_vb, w_s, b_s, w_out):
    B, S, _ = x.shape
    h = rms_norm(x, g_norm)
    z = h @ w_in
    q, k, v, qi, ki, wi, u, vb = split_columns(z, IN_SPLITS)
    q = rope(rms_norm(q.reshape(B, S, N_HEADS_A, HEAD_DIM), g_q), positions)
    k = rope(rms_norm(k, g_k)[:, :, None, :], positions)[:, :, 0]
    qi = rope(qi.reshape(B, S, N_IDX_HEADS, IDX_DIM), positions)
    ki = rope(rms_norm(ki, g_kidx)[:, :, None, :], positions)[:, :, 0]
    wi = wi * IDX_SCALE
    a_out = dsa_attention(q, k, v, qi, ki, wi)
    b_out = spatial_gating(jax.nn.gelu(jnp.concatenate([u, vb], axis=-1)), g_vb, w_s, b_s)
    return x + jnp.concatenate([a_out, b_out], axis=-1) @ w_out


def odd_layer(x, g_norm, w_pw1, w_dw, b_dw, g_conv, w_pw2):
    h = rms_norm(x, g_norm)
    z = h @ w_pw1
    y = z[..., :D_MODEL] * jax.nn.sigmoid(z[..., D_MODEL:])
    y = lax.conv_general_dilated(
        y, w_dw[:, None, :].astype(y.dtype), window_strides=(1,),
        padding=[(CONV_WIDTH - 1, 0)], dimension_numbers=('NWC', 'WIO', 'NWC'),
        feature_group_count=D_MODEL) + b_dw
    y = jax.nn.silu(rms_norm(y, g_conv))
    return x + y @ w_pw2


def channel_mixer(x, g_norm, w1, w2):
    h = rms_norm(x, g_norm)
    return x + jnp.square(jax.nn.relu(h @ w1)) @ w2


def setup_inputs(seed: int = 0) -> dict:
    key = jax.random.key(seed)
    ks = jax.random.split(key, 24)
    f32 = jnp.float32
    nrm = lambda k, shape, scale: jax.random.normal(k, shape, f32) * scale
    gain = lambda k, shape: 1.0 + 0.05 * jax.random.normal(k, shape, f32)
    x = jax.random.normal(ks[0], (BATCH, SEQ, D_MODEL), f32)
    offs = jax.random.randint(ks[1], (BATCH, 1), 0, 1024, dtype=jnp.int32)
    positions = (offs + jnp.arange(SEQ, dtype=jnp.int32)[None, :]).astype(jnp.int32)
    return {
        "x": x,
        "positions": positions,
        "ev_g_norm": gain(ks[2], (N_EVEN, D_MODEL)),
        "ev_w_in": nrm(ks[3], (N_EVEN, D_MODEL, IN_WIDTH), D_MODEL ** -0.5),
        "ev_g_q": gain(ks[4], (N_EVEN, HEAD_DIM)),
        "ev_g_k": gain(ks[5], (N_EVEN, HEAD_DIM)),
        "ev_g_kidx": gain(ks[6], (N_EVEN, IDX_DIM)),
        "ev_g_vb": gain(ks[7], (N_EVEN, WIDTH_B)),
        "ev_w_s": nrm(ks[8], (N_EVEN, N_GROUPS_B, CHUNK, CHUNK), 0.5 * CHUNK ** -0.5),
        "ev_b_s": 1.0 + nrm(ks[9], (N_EVEN, N_GROUPS_B, CHUNK), 0.05),
        "ev_w_out": nrm(ks[10], (N_EVEN, WIDTH_A + WIDTH_B, D_MODEL), (WIDTH_A + WIDTH_B) ** -0.5),
        "od_g_norm": gain(ks[11], (N_ODD, D_MODEL)),
        "od_w_pw1": nrm(ks[12], (N_ODD, D_MODEL, 2 * D_MODEL), D_MODEL ** -0.5),
        "od_w_dw": nrm(ks[13], (N_ODD, CONV_WIDTH, D_MODEL), CONV_WIDTH ** -0.5),
        "od_b_dw": nrm(ks[14], (N_ODD, D_MODEL), 0.01),
        "od_g_conv": gain(ks[15], (N_ODD, D_MODEL)),
        "od_w_pw2": nrm(ks[16], (N_ODD, D_MODEL, D_MODEL), D_MODEL ** -0.5),
        "ff_g_norm": gain(ks[17], (DEPTH, D_MODEL)),
        "ff_w1": nrm(ks[18], (DEPTH, D_MODEL, D_FF), D_MODEL ** -0.5),
        "ff_w2": nrm(ks[19], (DEPTH, D_FF, D_MODEL), 0.5 * D_FF ** -0.5),
    }


def reference(x, positions, ev_g_norm, ev_w_in, ev_g_q, ev_g_k, ev_g_kidx, ev_g_vb,
              ev_w_s, ev_b_s, ev_w_out, od_g_norm, od_w_pw1, od_w_dw, od_b_dw,
              od_g_conv, od_w_pw2, ff_g_norm, ff_w1, ff_w2):
    for layer in range(DEPTH):
        i = layer // 2
        if layer % 2 == 0:
            x = even_layer(x, positions, ev_g_norm[i], ev_w_in[i], ev_g_q[i], ev_g_k[i],
                           ev_g_kidx[i], ev_g_vb[i], ev_w_s[i], ev_b_s[i], ev_w_out[i])
        else:
            x = odd_layer(x, od_g_norm[i], od_w_pw1[i], od_w_dw[i], od_b_dw[i],
                          od_g_conv[i], od_w_pw2[i])
        x = channel_mixer(x, ff_g_norm[layer], ff_w1[layer], ff_w2[layer])
    return x
```

```python
import functools

import numpy as np
import jax
import jax.numpy as jnp
from jax import lax
from jax.experimental import pallas as pl
from jax.experimental.pallas import tpu as pltpu

F32 = jnp.float32
BF16 = jnp.bfloat16

D_MODEL = 1024
N_HEADS = 8
HEAD_DIM = 64
IDX_DIM = 32
TOPK = 256
Q_BLOCK = 128
N_GROUPS = 8
GROUP_DIM = 64
WIDTH_A = N_HEADS * HEAD_DIM
WIDTH_B = N_GROUPS * GROUP_DIM
CHUNK = 128
CONV_WIDTH = 31
D_FF = 4 * D_MODEL
ROPE_THETA = 10000.0
EPS = 1e-6
IDX_SCALE = (N_HEADS ** -0.5) * (IDX_DIM ** -0.5)

LANES = 128
KEY_CHUNK = 256
ROPE_COLS = (N_HEADS + 1) * LANES
MAIN_COLS = ROPE_COLS + LANES + 2 * WIDTH_B
CONV_HALO = 32
VMEM_LIMIT = 56 * 1024 * 1024


def _rms_rows(x, g):
    ms = jnp.mean(x * x, axis=-1, keepdims=True)
    return x * lax.rsqrt(ms + EPS) * g


def _sigmoid(x):
    return 1.0 / (1.0 + jnp.exp(-x))


def _gelu_tanh(x):
    c = np.float32(np.sqrt(2.0 / np.pi))
    return 0.5 * x * (1.0 + jnp.tanh(c * (x + 0.044715 * (x * x * x))))


def _ffn(x, g, w1_ref, w2_ref):
    h = _rms_rows(x, g).astype(BF16)
    acc = x
    slab = D_FF // 4
    for c in range(4):
        t = jnp.dot(h, w1_ref[:, c * slab:(c + 1) * slab], preferred_element_type=F32)
        t = jnp.maximum(t, 0.0)
        t = (t * t).astype(BF16)
        acc = acc + jnp.dot(t, w2_ref[c * slab:(c + 1) * slab, :], preferred_element_type=F32)
    return acc


def _in_stage_kernel(x_ref, pos_ref, gn_ref, wm_ref, ws_ref, freq_ref, sgn_ref,
                     gm_ref, gs_ref, vws_ref, gvb_ref, wsp_ref, bsp_ref,
                     r_ref, kk_ref, vt_ref, wt_ref, b_ref):
    tm = x_ref.shape[1]
    x = x_ref[0]
    h = _rms_rows(x, gn_ref[...]).astype(BF16)
    z = jnp.dot(h, wm_ref[...], preferred_element_type=F32)
    zs = jnp.dot(h, ws_ref[...], preferred_element_type=F32)

    lane = lax.broadcasted_iota(jnp.int32, (tm, LANES), 1)
    ang = pos_ref[0] * freq_ref[...]
    cos = jnp.cos(ang)
    sin = jnp.sin(ang) * sgn_ref[...]
    is_main = lane < HEAD_DIM

    for blk in range(N_HEADS + 1):
        sl = slice(blk * LANES, (blk + 1) * LANES)
        zb = z[:, sl]
        zw = zs[:, sl]
        sq = zb * zb
        ms_a = jnp.sum(jnp.where(is_main, sq, 0.0), axis=-1, keepdims=True) * (1.0 / HEAD_DIM)
        if blk < N_HEADS:
            inv = jnp.where(is_main, lax.rsqrt(ms_a + EPS), 1.0)
        else:
            ms_b = jnp.sum(jnp.where(is_main, 0.0, sq), axis=-1, keepdims=True) * (1.0 / IDX_DIM)
            inv = jnp.where(is_main, lax.rsqrt(ms_a + EPS), lax.rsqrt(ms_b + EPS))
        y = (zb * gm_ref[:, sl]) * inv
        yw = (zw * gs_ref[:, sl]) * inv
        out = (y * cos + yw * sin).astype(BF16)
        if blk < N_HEADS:
            r_ref[0, :, sl] = out
        else:
            kk_ref[0] = out

    vw = z[:, ROPE_COLS:ROPE_COLS + LANES] * vws_ref[...]
    vwt = vw.T
    for j in range(tm // KEY_CHUNK):
        vt_ref[0, j] = vwt[0:HEAD_DIM, j * KEY_CHUNK:(j + 1) * KEY_CHUNK].astype(BF16)
    wt_ref[0] = vwt[HEAD_DIM:HEAD_DIM + N_HEADS, :]

    u0 = ROPE_COLS + LANES
    ug = _gelu_tanh(z[:, u0:u0 + WIDTH_B])
    vg = _gelu_tanh(z[:, u0 + WIDTH_B:u0 + 2 * WIDTH_B])
    vn = _rms_rows(vg, gvb_ref[...]).astype(BF16)
    row = lax.broadcasted_iota(jnp.int32, (CHUNK, CHUNK), 0)
    col = lax.broadcasted_iota(jnp.int32, (CHUNK, CHUNK), 1)
    tril = col <= row
    wmix = [jnp.where(tril, wsp_ref[g], 0.0).astype(BF16) for g in range(N_GROUPS)]
    low = lax.broadcasted_iota(jnp.int32, (CHUNK, LANES), 1) < GROUP_DIM
    for c in range(tm // CHUNK):
        rows = slice(c * CHUNK, (c + 1) * CHUNK)
        for p in range(N_GROUPS // 2):
            cols = slice(p * LANES, (p + 1) * LANES)
            vb = vn[rows, cols]
            s0 = jnp.dot(wmix[2 * p], vb, preferred_element_type=F32)
            s1 = jnp.dot(wmix[2 * p + 1], vb, preferred_element_type=F32)
            sv = jnp.where(low, s0, s1) + bsp_ref[:, cols]
            b_ref[0, rows, cols] = (ug[rows, cols] * sv).astype(BF16)


def _in_stage(x, pos, gn, wm, ws, freq, sgn, gm, gs, vws, gvb, wsp, bsp, tm):
    B, S, D = x.shape
    n_kc = S // KEY_CHUNK
    const = lambda *shape: pl.BlockSpec(shape, lambda b, i: (0,) * len(shape),
                                        pipeline_mode=pl.Buffered(1))
    return pl.pallas_call(
        _in_stage_kernel,
        grid=(B, S // tm),
        in_specs=[
            pl.BlockSpec((1, tm, D), lambda b, i: (b, i, 0)),
            pl.BlockSpec((1, tm, 1), lambda b, i: (b, i, 0)),
            const(1, D),
            const(D, MAIN_COLS),
            const(D, ROPE_COLS),
            const(1, LANES),
            const(1, LANES),
            const(1, ROPE_COLS),
            const(1, ROPE_COLS),
            const(1, LANES),
            const(1, WIDTH_B),
            const(N_GROUPS, CHUNK, CHUNK),
            const(CHUNK, WIDTH_B),
        ],
        out_specs=[
            pl.BlockSpec((1, tm, N_HEADS * LANES), lambda b, i: (b, i, 0)),
            pl.BlockSpec((1, tm, LANES), lambda b, i: (b, i, 0)),
            pl.BlockSpec((1, tm // KEY_CHUNK, HEAD_DIM, KEY_CHUNK), lambda b, i: (b, i, 0, 0)),
            pl.BlockSpec((1, N_HEADS, tm), lambda b, i: (b, 0, i)),
            pl.BlockSpec((1, tm, WIDTH_B), lambda b, i: (b, i, 0)),
        ],
        out_shape=[
            jax.ShapeDtypeStruct((B, S, N_HEADS * LANES), BF16),
            jax.ShapeDtypeStruct((B, S, LANES), BF16),
            jax.ShapeDtypeStruct((B, n_kc, HEAD_DIM, KEY_CHUNK), BF16),
            jax.ShapeDtypeStruct((B, N_HEADS, S), F32),
            jax.ShapeDtypeStruct((B, S, WIDTH_B), BF16),
        ],
        compiler_params=pltpu.CompilerParams(
            dimension_semantics=("parallel", "parallel"), vmem_limit_bytes=VMEM_LIMIT),
        name="even_in_stage",
    )(x, pos, gn, wm, ws, freq, sgn, gm, gs, vws, gvb, wsp, bsp)


def _key_to_float(u):
    key = u ^ jnp.int32(-2 ** 31)
    bits = jnp.where(key >= 0, key, key ^ jnp.int32(0x7FFFFFFF))
    return lax.bitcast_convert_type(bits, jnp.float32)


def _dsa_kernel(r_ref, kk_ref, vt_ref, wt_ref, o_ref, sc_ref):
    qb = pl.program_id(1)
    n_ch = (qb + 2) // 2 if KEY_CHUNK == 2 * Q_BLOCK else (qb * Q_BLOCK + Q_BLOCK + KEY_CHUNK - 1) // KEY_CHUNK

    r = r_ref[0].astype(F32)
    lane = lax.broadcasted_iota(jnp.int32, (Q_BLOCK, LANES), 1)
    is_q = lane < HEAD_DIM
    rq = jnp.concatenate(
        [jnp.where(is_q, r[:, h * LANES:(h + 1) * LANES] * (HEAD_DIM ** -0.5), 0.0)
         for h in range(N_HEADS)], axis=0).astype(BF16)
    rqi = jnp.concatenate(
        [jnp.where(is_q, 0.0, r[:, h * LANES:(h + 1) * LANES]) for h in range(N_HEADS)],
        axis=0).astype(BF16)
    wt = wt_ref[0]
    t_idx = qb * Q_BLOCK + lax.broadcasted_iota(jnp.int32, (1, LANES), 1)
    row_iota = lax.broadcasted_iota(jnp.int32, (KEY_CHUNK, LANES), 0)
    nt = (((1,), (1,)), ((), ()))

    def score_body(c, carry):
        kk = kk_ref[0, c]
        lg = lax.dot_general(kk, rqi, nt, preferred_element_type=F32)
        sc = wt[0:1, :] * jnp.maximum(lg[:, 0:LANES], 0.0)
        for h in range(1, N_HEADS):
            sc = sc + wt[h:h + 1, :] * jnp.maximum(lg[:, h * LANES:(h + 1) * LANES], 0.0)
        s_idx = c * KEY_CHUNK + row_iota
        sc_ref[c] = jnp.where(s_idx <= t_idx, sc, -jnp.inf)
        return carry

    lax.fori_loop(0, n_ch, score_body, 0)

    k_row = jnp.minimum(t_idx + 1, TOPK).astype(F32)

    def count_ge(cvec):
        def body(c, acc):
            hit = jnp.where(sc_ref[c] >= cvec, 1.0, 0.0)
            return acc + hit.reshape(KEY_CHUNK // 8, 8, LANES).sum(axis=0)
        acc = lax.fori_loop(0, n_ch, body, jnp.zeros((8, LANES), F32))
        return acc.sum(axis=0, keepdims=True)

    def bit_body(i, carry):
        u, cnt_u = carry
        trial = u | lax.shift_left(jnp.int32(1), (31 - i).astype(jnp.int32))
        cnt = count_ge(_key_to_float(trial))
        take = cnt >= k_row
        return jnp.where(take, trial, u), jnp.where(take, cnt, cnt_u)

    u, cnt_thr = lax.fori_loop(
        0, 32, bit_body, (jnp.zeros((1, LANES), jnp.int32), jnp.zeros((1, LANES), F32)))
    thr = _key_to_float(u)

    @pl.when(jnp.max(cnt_thr - k_row) > 0.0)
    def _():
        def tie_count(c, acc):
            hit = jnp.where(sc_ref[c] == thr, 1.0, 0.0)
            return acc + hit.reshape(KEY_CHUNK // 8, 8, LANES).sum(axis=0)
        n_tie = lax.fori_loop(0, n_ch, tie_count, jnp.zeros((8, LANES), F32)).sum(
            axis=0, keepdims=True)
        need = k_row - (cnt_thr - n_tie)
        ri = lax.broadcasted_iota(jnp.int32, (KEY_CHUNK, KEY_CHUNK), 0)
        ci = lax.broadcasted_iota(jnp.int32, (KEY_CHUNK, KEY_CHUNK), 1)
        before = jnp.where(ci < ri, 1.0, 0.0).astype(BF16)

        def drop_body(c, seen):
            x = sc_ref[c]
            tie = x == thr
            tie_f = jnp.where(tie, 1.0, 0.0)
            rank = seen + jnp.dot(before, tie_f.astype(BF16), preferred_element_type=F32)
            sc_ref[c] = jnp.where(tie, jnp.where(rank >= need, -jnp.inf, x), x)
            return seen + jnp.sum(tie_f, axis=0, keepdims=True)

        lax.fori_loop(0, n_ch, drop_body, jnp.zeros((1, LANES), F32))

    def attn_body(c, carry):
        m, l, acc = carry
        kk = kk_ref[0, c]
        st = lax.dot_general(kk, rq, nt, preferred_element_type=F32)
        sel = sc_ref[c] >= thr
        ps, ms, ls = [], [], []
        for h in range(N_HEADS):
            hs = slice(h * LANES, (h + 1) * LANES)
            s = jnp.where(sel, st[:, hs], -jnp.inf)
            m_new = jnp.maximum(m[:, hs], jnp.max(s, axis=0, keepdims=True))
            p = jnp.exp(s - m_new)
            alpha = jnp.exp(m[:, hs] - m_new)
            ls.append(alpha * l[:, hs] + jnp.sum(p, axis=0, keepdims=True))
            ms.append(m_new)
            ps.append(p.astype(BF16))
        m2 = jnp.concatenate(ms, axis=1)
        alpha_all = jnp.exp(m - m2)
        pv = jnp.dot(vt_ref[0, c], jnp.concatenate(ps, axis=1), preferred_element_type=F32)
        return m2, jnp.concatenate(ls, axis=1), alpha_all * acc + pv

    m0 = jnp.full((1, N_HEADS * LANES), -1e30, F32)
    l0 = jnp.zeros((1, N_HEADS * LANES), F32)
    a0 = jnp.zeros((HEAD_DIM, N_HEADS * LANES), F32)
    _, l, acc = lax.fori_loop(0, n_ch, attn_body, (m0, l0, a0))
    out_t = acc * (1.0 / l)
    stacked = jnp.concatenate(
        [out_t[:, h * LANES:(h + 1) * LANES] for h in range(N_HEADS)], axis=0)
    o_ref[0] = stacked.T.astype(BF16)


def _dsa(r, kk, vt, wt):
    B, S, _ = r.shape
    n_kc = S // KEY_CHUNK
    return pl.pallas_call(
        _dsa_kernel,
        grid=(B, S // Q_BLOCK),
        in_specs=[
            pl.BlockSpec((1, Q_BLOCK, N_HEADS * LANES), lambda b, i: (b, i, 0)),
            pl.BlockSpec((1, n_kc, KEY_CHUNK, LANES), lambda b, i: (b, 0, 0, 0)),
            pl.BlockSpec((1, n_kc, HEAD_DIM, KEY_CHUNK), lambda b, i: (b, 0, 0, 0)),
            pl.BlockSpec((1, N_HEADS, Q_BLOCK), lambda b, i: (b, 0, i)),
        ],
        out_specs=pl.BlockSpec((1, Q_BLOCK, WIDTH_A), lambda b, i: (b, i, 0)),
        out_shape=jax.ShapeDtypeStruct((B, S, WIDTH_A), BF16),
        scratch_shapes=[pltpu.VMEM((n_kc, KEY_CHUNK, LANES), F32)],
        compiler_params=pltpu.CompilerParams(
            dimension_semantics=("parallel", "parallel"), vmem_limit_bytes=VMEM_LIMIT),
        name="dsa_attention",
    )(r, kk, vt, wt)


def _out_ffn_kernel(x_ref, a_ref, b_ref, wo_ref, g_ref, w1_ref, w2_ref, o_ref):
    ab = jnp.concatenate([a_ref[...], b_ref[...]], axis=1)
    o_ref[...] = x_ref[...] + jnp.dot(ab, wo_ref[...], preferred_element_type=F32)
    o_ref[...] = _ffn(o_ref[...], g_ref[...], w1_ref, w2_ref)


def _out_ffn(x, a, b, wo, g, w1, w2, tm):
    T, D = x.shape
    const = lambda *shape: pl.BlockSpec(shape, lambda i: (0,) * len(shape),
                                        pipeline_mode=pl.Buffered(1))
    return pl.pallas_call(
        _out_ffn_kernel,
        grid=(T // tm,),
        in_specs=[
            pl.BlockSpec((tm, D), lambda i: (i, 0)),
            pl.BlockSpec((tm, WIDTH_A), lambda i: (i, 0)),
            pl.BlockSpec((tm, WIDTH_B), lambda i: (i, 0)),
            const(WIDTH_A + WIDTH_B, D),
            const(1, D),
            const(D, D_FF),
            const(D_FF, D),
        ],
        out_specs=pl.BlockSpec((tm, D), lambda i: (i, 0)),
        out_shape=jax.ShapeDtypeStruct((T, D), F32),
        compiler_params=pltpu.CompilerParams(
            dimension_semantics=("parallel",), vmem_limit_bytes=VMEM_LIMIT),
        name="out_proj_ffn",
    )(x, a, b, wo, g, w1, w2)


def _conv_ffn_kernel(x_ref, gn_ref, wp1_ref, wdw_ref, bdw_ref, gc_ref, wp2_ref,
                     g_ref, w1_ref, w2_ref, o_ref, ybuf):
    tm = x_ref.shape[1]
    x = x_ref[0]
    h = _rms_rows(x, gn_ref[...]).astype(BF16)
    z = jnp.dot(h, wp1_ref[...], preferred_element_type=F32)
    y = z[:, :D_MODEL] * _sigmoid(z[:, D_MODEL:])

    @pl.when(pl.program_id(1) == 0)
    def _():
        ybuf[0:CONV_HALO, :] = jnp.zeros((CONV_HALO, D_MODEL), F32)

    ybuf[CONV_HALO:CONV_HALO + tm, :] = y
    first = CONV_HALO - (CONV_WIDTH - 1)
    acc = bdw_ref[...] + wdw_ref[0:1, :] * ybuf[first:first + tm, :]
    for j in range(1, CONV_WIDTH):
        acc = acc + wdw_ref[j:j + 1, :] * ybuf[first + j:first + j + tm, :]
    ybuf[0:CONV_HALO, :] = ybuf[tm:tm + CONV_HALO, :]

    c = _rms_rows(acc, gc_ref[...])
    c = (c * _sigmoid(c)).astype(BF16)
    o_ref[0] = x + jnp.dot(c, wp2_ref[...], preferred_element_type=F32)
    o_ref[0] = _ffn(o_ref[0], g_ref[...], w1_ref, w2_ref)


def _conv_ffn(x, gn, wp1, wdw, bdw, gc, wp2, g, w1, w2, tm):
    B, S, D = x.shape
    const = lambda *shape: pl.BlockSpec(shape, lambda b, i: (0,) * len(shape),
                                        pipeline_mode=pl.Buffered(1))
    return pl.pallas_call(
        _conv_ffn_kernel,
        grid=(B, S // tm),
        in_specs=[
            pl.BlockSpec((1, tm, D), lambda b, i: (b, i, 0)),
            const(1, D),
            const(D, 2 * D),
            const(CONV_WIDTH, D),
            const(1, D),
            const(1, D),
            const(D, D),
            const(1, D),
            const(D, D_FF),
            const(D_FF, D),
        ],
        out_specs=pl.BlockSpec((1, tm, D), lambda b, i: (b, i, 0)),
        out_shape=jax.ShapeDtypeStruct((B, S, D), F32),
        scratch_shapes=[pltpu.VMEM((tm + CONV_HALO, D), F32)],
        compiler_params=pltpu.CompilerParams(
            dimension_semantics=("arbitrary", "arbitrary"), vmem_limit_bytes=VMEM_LIMIT),
        name="conv_module_ffn",
    )(x, gn, wp1, wdw, bdw, gc, wp2, g, w1, w2)


def _column_plan():
    q0, k0, v0 = 0, WIDTH_A, WIDTH_A + HEAD_DIM
    qi0 = v0 + HEAD_DIM
    ki0 = qi0 + N_HEADS * IDX_DIM
    wi0 = ki0 + IDX_DIM
    u0 = wi0 + N_HEADS
    vb0 = u0 + WIDTH_B
    main = -np.ones((MAIN_COLS,), np.int64)
    swap = -np.ones((ROPE_COLS,), np.int64)

    def put(dst, src, width):
        half = width // 2
        for d in range(width):
            main[dst + d] = src + d
            swap[dst + d] = src + (d + half) % width

    for h in range(N_HEADS):
        put(h * LANES, q0 + h * HEAD_DIM, HEAD_DIM)
        put(h * LANES + HEAD_DIM, qi0 + h * IDX_DIM, IDX_DIM)
    put(N_HEADS * LANES, k0, HEAD_DIM)
    put(N_HEADS * LANES + HEAD_DIM, ki0, IDX_DIM)
    main[ROPE_COLS:ROPE_COLS + HEAD_DIM] = v0 + np.arange(HEAD_DIM)
    main[ROPE_COLS + HEAD_DIM:ROPE_COLS + HEAD_DIM + N_HEADS] = wi0 + np.arange(N_HEADS)
    main[ROPE_COLS + LANES:ROPE_COLS + LANES + WIDTH_B] = u0 + np.arange(WIDTH_B)
    main[ROPE_COLS + LANES + WIDTH_B:] = vb0 + np.arange(WIDTH_B)
    return main, swap


def _take_cols(w, idx):
    wz = jnp.concatenate([w, jnp.zeros((w.shape[0], 1), w.dtype)], axis=1)
    return jnp.take(wz, jnp.asarray(np.where(idx < 0, w.shape[1], idx)), axis=1)


def _rope_lane_tables(g_q, g_k, g_kidx):
    half_q, half_i = HEAD_DIM // 2, IDX_DIM // 2
    f_q = ROPE_THETA ** (-jnp.arange(half_q, dtype=F32) * 2.0 / HEAD_DIM)
    f_i = ROPE_THETA ** (-jnp.arange(half_i, dtype=F32) * 2.0 / IDX_DIM)
    pad = jnp.zeros((LANES - HEAD_DIM - IDX_DIM,), F32)
    freq = jnp.concatenate([f_q, f_q, f_i, f_i, pad])
    sgn = jnp.concatenate([-jnp.ones((half_q,), F32), jnp.ones((half_q,), F32),
                           -jnp.ones((half_i,), F32), jnp.ones((half_i,), F32), pad])
    swap_q = jnp.concatenate([g_q[half_q:], g_q[:half_q]])
    swap_k = jnp.concatenate([g_k[half_q:], g_k[:half_q]])
    swap_i = jnp.concatenate([g_kidx[half_i:], g_kidx[:half_i]])
    ones_i = jnp.ones((IDX_DIM,), F32)
    q_main = jnp.concatenate([g_q, ones_i, pad])
    q_swap = jnp.concatenate([swap_q, ones_i, pad])
    k_main = jnp.concatenate([g_k, g_kidx, pad])
    k_swap = jnp.concatenate([swap_k, swap_i, pad])
    gm = jnp.concatenate([jnp.tile(q_main, N_HEADS), k_main])
    gs = jnp.concatenate([jnp.tile(q_swap, N_HEADS), k_swap])
    return freq[None], sgn[None], gm[None], gs[None]


def kernel(x, positions, ev_g_norm, ev_w_in, ev_g_q, ev_g_k, ev_g_kidx, ev_g_vb, ev_w_s, ev_b_s,
           ev_w_out, od_g_norm, od_w_pw1, od_w_dw, od_b_dw, od_g_conv, od_w_pw2, ff_g_norm,
           ff_w1, ff_w2):
    B, S, D = x.shape
    assert D == D_MODEL and S % 512 == 0
    tm = 512

    main_idx, swap_idx = _column_plan()
    w_in = ev_w_in[0]
    wm = _take_cols(w_in, main_idx).astype(BF16)
    ws = _take_cols(w_in, swap_idx).astype(BF16)
    freq, sgn, gm, gs = _rope_lane_tables(ev_g_q[0], ev_g_k[0], ev_g_kidx[0])
    vws = jnp.concatenate([jnp.ones((HEAD_DIM,), F32), jnp.full((N_HEADS,), IDX_SCALE, F32),
                           jnp.zeros((LANES - HEAD_DIM - N_HEADS,), F32)])[None]
    bsp = jnp.repeat(ev_b_s[0].T, GROUP_DIM, axis=1)
    pos = positions.astype(F32)[..., None]

    r, kk, vt, wt, b_out = _in_stage(
        x, pos, ev_g_norm[0][None], wm, ws, freq, sgn, gm, gs, vws, ev_g_vb[0][None],
        ev_w_s[0], bsp, tm)
    a_out = _dsa(r, kk.reshape(B, S // KEY_CHUNK, KEY_CHUNK, LANES), vt, wt)
    x1 = _out_ffn(x.reshape(B * S, D), a_out.reshape(B * S, WIDTH_A), b_out.reshape(B * S, WIDTH_B),
                  ev_w_out[0].astype(BF16), ff_g_norm[0][None], ff_w1[0].astype(BF16),
                  ff_w2[0].astype(BF16), tm)

    x2 = _conv_ffn(x1.reshape(B, S, D), od_g_norm[0][None], od_w_pw1[0].astype(BF16), od_w_dw[0],
                   od_b_dw[0][None], od_g_conv[0][None], od_w_pw2[0].astype(BF16),
                   ff_g_norm[1][None], ff_w1[1].astype(BF16), ff_w2[1].astype(BF16), tm)
    return x2
```

```python
import functools

import numpy as np
import jax
import jax.numpy as jnp
from jax import lax
from jax.experimental import pallas as pl
from jax.experimental.pallas import tpu as pltpu

F32 = jnp.float32
BF16 = jnp.bfloat16

D_MODEL = 1024
N_HEADS = 8
HEAD_DIM = 64
IDX_DIM = 32
TOPK = 256
Q_BLOCK = 128
N_GROUPS = 8
GROUP_DIM = 64
WIDTH_A = N_HEADS * HEAD_DIM
WIDTH_B = N_GROUPS * GROUP_DIM
CHUNK = 128
CONV_WIDTH = 31
D_FF = 4 * D_MODEL
ROPE_THETA = 10000.0
EPS = 1e-6
IDX_SCALE = (N_HEADS ** -0.5) * (IDX_DIM ** -0.5)

LANES = 128
KEY_CHUNK = 512
LOG2_E = 1.4426950408889634
COUNT_ROWS = 64
V_ROWS = HEAD_DIM + 16
ROPE_COLS = (N_HEADS + 1) * LANES
MAIN_COLS = ROPE_COLS + LANES + 2 * WIDTH_B
CONV_HALO = 32
VMEM_LIMIT = 56 * 1024 * 1024


def _rms_rows(x, g):
    ms = jnp.mean(x * x, axis=-1, keepdims=True)
    return x * lax.rsqrt(ms + EPS) * g


def _sigmoid(x):
    return 1.0 / (1.0 + jnp.exp(-x))


def _gelu_tanh(x):
    c = np.float32(np.sqrt(2.0 / np.pi))
    return 0.5 * x * (1.0 + jnp.tanh(c * (x + 0.044715 * (x * x * x))))


def _ffn(x, g, w1_ref, w2_ref):
    h = _rms_rows(x, g).astype(BF16)
    acc = x
    slab = D_FF // 4
    for c in range(4):
        t = jnp.dot(h, w1_ref[:, c * slab:(c + 1) * slab], preferred_element_type=F32)
        t = jnp.maximum(t, 0.0)
        t = (t * t).astype(BF16)
        acc = acc + jnp.dot(t, w2_ref[c * slab:(c + 1) * slab, :], preferred_element_type=F32)
    return acc


def _in_stage_kernel(x_ref, pos_ref, gn_ref, wm_ref, ws_ref, freq_ref, sgn_ref,
                     gm_ref, gs_ref, vws_ref, gvb_ref, wsp_ref, bsp_ref,
                     r_ref, kk_ref, vt_ref, wt_ref, b_ref):
    tm = x_ref.shape[1]
    x = x_ref[0]
    h = _rms_rows(x, gn_ref[...]).astype(BF16)
    z = jnp.dot(h, wm_ref[...], preferred_element_type=F32)
    zs = jnp.dot(h, ws_ref[...], preferred_element_type=F32)

    lane = lax.broadcasted_iota(jnp.int32, (tm, LANES), 1)
    ang = pos_ref[0] * freq_ref[...]
    cos = jnp.cos(ang)
    sin = jnp.sin(ang) * sgn_ref[...]
    is_main = lane < HEAD_DIM

    for blk in range(N_HEADS + 1):
        sl = slice(blk * LANES, (blk + 1) * LANES)
        zb = z[:, sl]
        zw = zs[:, sl]
        sq = zb * zb
        ms_a = jnp.sum(jnp.where(is_main, sq, 0.0), axis=-1, keepdims=True) * (1.0 / HEAD_DIM)
        if blk < N_HEADS:
            inv = jnp.where(is_main, lax.rsqrt(ms_a + EPS), 1.0)
        else:
            ms_b = jnp.sum(jnp.where(is_main, 0.0, sq), axis=-1, keepdims=True) * (1.0 / IDX_DIM)
            inv = jnp.where(is_main, lax.rsqrt(ms_a + EPS), lax.rsqrt(ms_b + EPS))
        y = (zb * gm_ref[:, sl]) * inv
        yw = (zw * gs_ref[:, sl]) * inv
        out = (y * cos + yw * sin).astype(BF16)
        if blk < N_HEADS:
            r_ref[0, :, sl] = out
        else:
            kk_ref[0] = out

    vw = z[:, ROPE_COLS:ROPE_COLS + LANES] * vws_ref[...]
    vwt = vw.T
    for j in range(tm // KEY_CHUNK):
        vt_ref[0, j] = jnp.concatenate(
            [vwt[0:HEAD_DIM, j * KEY_CHUNK:(j + 1) * KEY_CHUNK],
             jnp.ones((V_ROWS - HEAD_DIM, KEY_CHUNK), F32)], axis=0).astype(BF16)
    wt_ref[0] = vwt[HEAD_DIM:HEAD_DIM + N_HEADS, :]

    u0 = ROPE_COLS + LANES
    ug = _gelu_tanh(z[:, u0:u0 + WIDTH_B])
    vg = _gelu_tanh(z[:, u0 + WIDTH_B:u0 + 2 * WIDTH_B])
    vn = _rms_rows(vg, gvb_ref[...]).astype(BF16)
    row = lax.broadcasted_iota(jnp.int32, (CHUNK, CHUNK), 0)
    col = lax.broadcasted_iota(jnp.int32, (CHUNK, CHUNK), 1)
    tril = col <= row
    wmix = [jnp.where(tril, wsp_ref[g], 0.0).astype(BF16) for g in range(N_GROUPS)]
    low = lax.broadcasted_iota(jnp.int32, (CHUNK, LANES), 1) < GROUP_DIM
    for c in range(tm // CHUNK):
        rows = slice(c * CHUNK, (c + 1) * CHUNK)
        for p in range(N_GROUPS // 2):
            cols = slice(p * LANES, (p + 1) * LANES)
            vb = vn[rows, cols]
            s0 = jnp.dot(wmix[2 * p], vb, preferred_element_type=F32)
            s1 = jnp.dot(wmix[2 * p + 1], vb, preferred_element_type=F32)
            sv = jnp.where(low, s0, s1) + bsp_ref[:, cols]
            b_ref[0, rows, cols] = (ug[rows, cols] * sv).astype(BF16)


def _in_stage(x, pos, gn, wm, ws, freq, sgn, gm, gs, vws, gvb, wsp, bsp, tm):
    B, S, D = x.shape
    n_kc = S // KEY_CHUNK
    const = lambda *shape: pl.BlockSpec(shape, lambda b, i: (0,) * len(shape),
                                        pipeline_mode=pl.Buffered(1))
    return pl.pallas_call(
        _in_stage_kernel,
        grid=(B, S // tm),
        in_specs=[
            pl.BlockSpec((1, tm, D), lambda b, i: (b, i, 0)),
            pl.BlockSpec((1, tm, 1), lambda b, i: (b, i, 0)),
            const(1, D),
            const(D, MAIN_COLS),
            const(D, ROPE_COLS),
            const(1, LANES),
            const(1, LANES),
            const(1, ROPE_COLS),
            const(1, ROPE_COLS),
            const(1, LANES),
            const(1, WIDTH_B),
            const(N_GROUPS, CHUNK, CHUNK),
            const(CHUNK, WIDTH_B),
        ],
        out_specs=[
            pl.BlockSpec((1, tm, N_HEADS * LANES), lambda b, i: (b, i, 0)),
            pl.BlockSpec((1, tm, LANES), lambda b, i: (b, i, 0)),
            pl.BlockSpec((1, tm // KEY_CHUNK, V_ROWS, KEY_CHUNK), lambda b, i: (b, i, 0, 0)),
            pl.BlockSpec((1, N_HEADS, tm), lambda b, i: (b, 0, i)),
            pl.BlockSpec((1, tm, WIDTH_B), lambda b, i: (b, i, 0)),
        ],
        out_shape=[
            jax.ShapeDtypeStruct((B, S, N_HEADS * LANES), BF16),
            jax.ShapeDtypeStruct((B, S, LANES), BF16),
            jax.ShapeDtypeStruct((B, n_kc, V_ROWS, KEY_CHUNK), BF16),
            jax.ShapeDtypeStruct((B, N_HEADS, S), F32),
            jax.ShapeDtypeStruct((B, S, WIDTH_B), BF16),
        ],
        compiler_params=pltpu.CompilerParams(
            dimension_semantics=("parallel", "parallel"), vmem_limit_bytes=VMEM_LIMIT),
        name="even_in_stage",
    )(x, pos, gn, wm, ws, freq, sgn, gm, gs, vws, gvb, wsp, bsp)


def _key_to_float(u):
    key = u ^ jnp.int32(-2 ** 31)
    bits = jnp.where(key >= 0, key, key ^ jnp.int32(0x7FFFFFFF))
    return lax.bitcast_convert_type(bits, jnp.float32)


def _dsa_kernel(r_ref, kk_ref, vt_ref, wt_ref, o_ref, sc_ref, buf_a, buf_b):
    qb = pl.program_id(1)
    n_ch = (qb * Q_BLOCK + Q_BLOCK + KEY_CHUNK - 1) // KEY_CHUNK
    n_pair = (n_ch + 1) // 2

    r = r_ref[0].astype(F32)
    lane = lax.broadcasted_iota(jnp.int32, (Q_BLOCK, LANES), 1)
    is_q = lane < HEAD_DIM
    q_scale = (HEAD_DIM ** -0.5) * LOG2_E
    rq = jnp.concatenate(
        [jnp.where(is_q, r[:, h * LANES:(h + 1) * LANES] * q_scale, 0.0)
         for h in range(N_HEADS)], axis=0).astype(BF16)
    rqi = jnp.concatenate(
        [jnp.where(is_q, 0.0, r[:, h * LANES:(h + 1) * LANES]) for h in range(N_HEADS)],
        axis=0).astype(BF16)
    wt = wt_ref[0]
    t_idx = qb * Q_BLOCK + lax.broadcasted_iota(jnp.int32, (1, LANES), 1)
    row_iota = lax.broadcasted_iota(jnp.int32, (KEY_CHUNK, LANES), 0)
    nt = (((1,), (1,)), ((), ()))

    last = kk_ref.shape[1] - 1

    def keys_dot(c, w):
        return lax.dot_general(kk_ref[0, jnp.minimum(c, last)], w, nt, preferred_element_type=F32)


    def write_scores(c, lg_ref):
        sc = wt[0:1, :] * jnp.maximum(lg_ref[:, 0:LANES], 0.0)
        for h in range(1, N_HEADS):
            sc = sc + wt[h:h + 1, :] * jnp.maximum(lg_ref[:, h * LANES:(h + 1) * LANES], 0.0)
        s_idx = c * KEY_CHUNK + row_iota
        sc_ref[c] = jnp.where(s_idx <= t_idx, sc, -jnp.inf)

    def score_pair(j, carry):
        buf_b[...] = keys_dot(2 * j + 1, rqi)
        write_scores(2 * j, buf_a)
        buf_a[...] = keys_dot(2 * j + 2, rqi)
        write_scores(2 * j + 1, buf_b)
        return carry

    buf_a[...] = keys_dot(0, rqi)
    lax.fori_loop(0, n_pair, score_pair, 0)

    k_row = jnp.minimum(t_idx + 1, TOPK).astype(F32)

    def count_rows(pred):
        def body(j, acc):
            x = sc_ref[pl.ds(2 * j, 2)]
            hit = jnp.where(pred(x), 1.0, 0.0)
            return acc + hit.reshape(2 * KEY_CHUNK // COUNT_ROWS, COUNT_ROWS, LANES).sum(axis=0)
        acc = lax.fori_loop(0, n_pair, body, jnp.zeros((COUNT_ROWS, LANES), F32))
        return acc.sum(axis=0, keepdims=True)

    def bit_body(i, carry):
        u, cnt_u = carry
        trial = u | lax.shift_left(jnp.int32(1), (31 - i).astype(jnp.int32))
        cvec = _key_to_float(trial)
        cnt = count_rows(lambda x: x >= cvec)
        take = cnt >= k_row
        return jnp.where(take, trial, u), jnp.where(take, cnt, cnt_u)

    u, cnt_thr = lax.fori_loop(
        0, 32, bit_body, (jnp.zeros((1, LANES), jnp.int32), jnp.zeros((1, LANES), F32)))
    thr = _key_to_float(u)

    @pl.when(jnp.max(cnt_thr - k_row) > 0.0)
    def _():
        n_tie = count_rows(lambda x: x == thr)
        need = k_row - (cnt_thr - n_tie)
        ri = lax.broadcasted_iota(jnp.int32, (KEY_CHUNK, KEY_CHUNK), 0)
        ci = lax.broadcasted_iota(jnp.int32, (KEY_CHUNK, KEY_CHUNK), 1)
        before = jnp.where(ci < ri, 1.0, 0.0).astype(BF16)

        def drop_body(c, seen):
            x = sc_ref[c]
            tie = x == thr
            tie_f = jnp.where(tie, 1.0, 0.0)
            rank = seen + jnp.dot(before, tie_f.astype(BF16), preferred_element_type=F32)
            sc_ref[c] = jnp.where(tie, jnp.where(rank >= need, -jnp.inf, x), x)
            return seen + jnp.sum(tie_f, axis=0, keepdims=True)

        lax.fori_loop(0, n_ch, drop_body, jnp.zeros((1, LANES), F32))

    def attend(c, st_ref, carry):
        ms, accs = carry
        vt = vt_ref[0, c]
        sel = sc_ref[c] >= thr
        new_ms, new_accs = [], []
        for pr in range(N_HEADS // 2):
            ps, alphas = [], []
            for hh in range(2):
                h = 2 * pr + hh
                m_old = ms[h]
                s = jnp.where(sel, st_ref[:, h * LANES:(h + 1) * LANES], -jnp.inf)
                m_new = jnp.maximum(m_old, jnp.max(s, axis=0, keepdims=True))
                ps.append(jnp.exp2(s - m_new).astype(BF16))
                alphas.append(jnp.exp2(m_old - m_new))
                new_ms.append(m_new)
            pv = jnp.dot(vt, jnp.concatenate(ps, axis=1), preferred_element_type=F32)
            new_accs.append(jnp.concatenate(alphas, axis=1) * accs[pr] + pv)
        return tuple(new_ms), tuple(new_accs)

    def attn_pair(j, carry):
        buf_b[...] = keys_dot(2 * j + 1, rq)
        carry = attend(2 * j, buf_a, carry)
        buf_a[...] = keys_dot(2 * j + 2, rq)
        return attend(2 * j + 1, buf_b, carry)

    ms0 = tuple(jnp.full((1, LANES), -1e30, F32) for _ in range(N_HEADS))
    accs0 = tuple(jnp.zeros((V_ROWS, 2 * LANES), F32) for _ in range(N_HEADS // 2))
    buf_a[...] = keys_dot(0, rq)
    _, accs = lax.fori_loop(0, n_pair, attn_pair, (ms0, accs0))
    outs = []
    for h in range(N_HEADS):
        a = accs[h // 2][:, (h % 2) * LANES:(h % 2 + 1) * LANES]
        outs.append(a[0:HEAD_DIM] * (1.0 / a[HEAD_DIM:HEAD_DIM + 1]))
    o_ref[0] = jnp.concatenate(outs, axis=0).T.astype(BF16)


def _dsa(r, kk, vt, wt):
    B, S, _ = r.shape
    n_kc = S // KEY_CHUNK
    return pl.pallas_call(
        _dsa_kernel,
        grid=(B, S // Q_BLOCK),
        in_specs=[
            pl.BlockSpec((1, Q_BLOCK, N_HEADS * LANES), lambda b, i: (b, i, 0)),
            pl.BlockSpec((1, n_kc, KEY_CHUNK, LANES), lambda b, i: (b, 0, 0, 0)),
            pl.BlockSpec((1, n_kc, V_ROWS, KEY_CHUNK), lambda b, i: (b, 0, 0, 0)),
            pl.BlockSpec((1, N_HEADS, Q_BLOCK), lambda b, i: (b, 0, i)),
        ],
        out_specs=pl.BlockSpec((1, Q_BLOCK, WIDTH_A), lambda b, i: (b, i, 0)),
        out_shape=jax.ShapeDtypeStruct((B, S, WIDTH_A), BF16),
        scratch_shapes=[pltpu.VMEM((n_kc + 1, KEY_CHUNK, LANES), F32),
                        pltpu.VMEM((KEY_CHUNK, N_HEADS * LANES), F32),
                        pltpu.VMEM((KEY_CHUNK, N_HEADS * LANES), F32)],
        compiler_params=pltpu.CompilerParams(
            dimension_semantics=("parallel", "parallel"), vmem_limit_bytes=VMEM_LIMIT),
        name="dsa_attention",
    )(r, kk, vt, wt)


def _out_ffn_kernel(x_ref, a_ref, b_ref, wo_ref, g_ref, w1_ref, w2_ref, o_ref):
    ab = jnp.concatenate([a_ref[...], b_ref[...]], axis=1)
    o_ref[...] = x_ref[...] + jnp.dot(ab, wo_ref[...], preferred_element_type=F32)
    o_ref[...] = _ffn(o_ref[...], g_ref[...], w1_ref, w2_ref)


def _out_ffn(x, a, b, wo, g, w1, w2, tm):
    T, D = x.shape
    const = lambda *shape: pl.BlockSpec(shape, lambda i: (0,) * len(shape),
                                        pipeline_mode=pl.Buffered(1))
    return pl.pallas_call(
        _out_ffn_kernel,
        grid=(T // tm,),
        in_specs=[
            pl.BlockSpec((tm, D), lambda i: (i, 0)),
            pl.BlockSpec((tm, WIDTH_A), lambda i: (i, 0)),
            pl.BlockSpec((tm, WIDTH_B), lambda i: (i, 0)),
            const(WIDTH_A + WIDTH_B, D),
            const(1, D),
            const(D, D_FF),
            const(D_FF, D),
        ],
        out_specs=pl.BlockSpec((tm, D), lambda i: (i, 0)),
        out_shape=jax.ShapeDtypeStruct((T, D), F32),
        compiler_params=pltpu.CompilerParams(
            dimension_semantics=("parallel",), vmem_limit_bytes=VMEM_LIMIT),
        name="out_proj_ffn",
    )(x, a, b, wo, g, w1, w2)


def _conv_ffn_kernel(x_ref, gn_ref, wp1_ref, wdw_ref, bdw_ref, gc_ref, wp2_ref,
                     g_ref, w1_ref, w2_ref, o_ref, ybuf):
    tm = x_ref.shape[1]
    x = x_ref[0]
    h = _rms_rows(x, gn_ref[...]).astype(BF16)
    z = jnp.dot(h, wp1_ref[...], preferred_element_type=F32)
    y = z[:, :D_MODEL] * _sigmoid(z[:, D_MODEL:])

    @pl.when(pl.program_id(1) == 0)
    def _():
        ybuf[0:CONV_HALO, :] = jnp.zeros((CONV_HALO, D_MODEL), F32)

    ybuf[CONV_HALO:CONV_HALO + tm, :] = y
    first = CONV_HALO - (CONV_WIDTH - 1)
    acc = bdw_ref[...] + wdw_ref[0:1, :] * ybuf[first:first + tm, :]
    for j in range(1, CONV_WIDTH):
        acc = acc + wdw_ref[j:j + 1, :] * ybuf[first + j:first + j + tm, :]
    ybuf[0:CONV_HALO, :] = ybuf[tm:tm + CONV_HALO, :]

    c = _rms_rows(acc, gc_ref[...])
    c = (c * _sigmoid(c)).astype(BF16)
    o_ref[0] = x + jnp.dot(c, wp2_ref[...], preferred_element_type=F32)
    o_ref[0] = _ffn(o_ref[0], g_ref[...], w1_ref, w2_ref)


def _conv_ffn(x, gn, wp1, wdw, bdw, gc, wp2, g, w1, w2, tm):
    B, S, D = x.shape
    const = lambda *shape: pl.BlockSpec(shape, lambda b, i: (0,) * len(shape),
                                        pipeline_mode=pl.Buffered(1))
    return pl.pallas_call(
        _conv_ffn_kernel,
        grid=(B, S // tm),
        in_specs=[
            pl.BlockSpec((1, tm, D), lambda b, i: (b, i, 0)),
            const(1, D),
            const(D, 2 * D),
            const(CONV_WIDTH, D),
            const(1, D),
            const(1, D),
            const(D, D),
            const(1, D),
            const(D, D_FF),
            const(D_FF, D),
        ],
        out_specs=pl.BlockSpec((1, tm, D), lambda b, i: (b, i, 0)),
        out_shape=jax.ShapeDtypeStruct((B, S, D), F32),
        scratch_shapes=[pltpu.VMEM((tm + CONV_HALO, D), F32)],
        compiler_params=pltpu.CompilerParams(
            dimension_semantics=("arbitrary", "arbitrary"), vmem_limit_bytes=VMEM_LIMIT),
        name="conv_module_ffn",
    )(x, gn, wp1, wdw, bdw, gc, wp2, g, w1, w2)


def _column_plan():
    q0, k0, v0 = 0, WIDTH_A, WIDTH_A + HEAD_DIM
    qi0 = v0 + HEAD_DIM
    ki0 = qi0 + N_HEADS * IDX_DIM
    wi0 = ki0 + IDX_DIM
    u0 = wi0 + N_HEADS
    vb0 = u0 + WIDTH_B
    main = -np.ones((MAIN_COLS,), np.int64)
    swap = -np.ones((ROPE_COLS,), np.int64)

    def put(dst, src, width):
        half = width // 2
        for d in range(width):
            main[dst + d] = src + d
            swap[dst + d] = src + (d + half) % width

    for h in range(N_HEADS):
        put(h * LANES, q0 + h * HEAD_DIM, HEAD_DIM)
        put(h * LANES + HEAD_DIM, qi0 + h * IDX_DIM, IDX_DIM)
    put(N_HEADS * LANES, k0, HEAD_DIM)
    put(N_HEADS * LANES + HEAD_DIM, ki0, IDX_DIM)
    main[ROPE_COLS:ROPE_COLS + HEAD_DIM] = v0 + np.arange(HEAD_DIM)
    main[ROPE_COLS + HEAD_DIM:ROPE_COLS + HEAD_DIM + N_HEADS] = wi0 + np.arange(N_HEADS)
    main[ROPE_COLS + LANES:ROPE_COLS + LANES + WIDTH_B] = u0 + np.arange(WIDTH_B)
    main[ROPE_COLS + LANES + WIDTH_B:] = vb0 + np.arange(WIDTH_B)
    return main, swap


def _take_cols(w, idx):
    wz = jnp.concatenate([w, jnp.zeros((w.shape[0], 1), w.dtype)], axis=1)
    return jnp.take(wz, jnp.asarray(np.where(idx < 0, w.shape[1], idx)), axis=1)


def _rope_lane_tables(g_q, g_k, g_kidx):
    half_q, half_i = HEAD_DIM // 2, IDX_DIM // 2
    f_q = ROPE_THETA ** (-jnp.arange(half_q, dtype=F32) * 2.0 / HEAD_DIM)
    f_i = ROPE_THETA ** (-jnp.arange(half_i, dtype=F32) * 2.0 / IDX_DIM)
    pad = jnp.zeros((LANES - HEAD_DIM - IDX_DIM,), F32)
    freq = jnp.concatenate([f_q, f_q, f_i, f_i, pad])
    sgn = jnp.concatenate([-jnp.ones((half_q,), F32), jnp.ones((half_q,), F32),
                           -jnp.ones((half_i,), F32), jnp.ones((half_i,), F32), pad])
    swap_q = jnp.concatenate([g_q[half_q:], g_q[:half_q]])
    swap_k = jnp.concatenate([g_k[half_q:], g_k[:half_q]])
    swap_i = jnp.concatenate([g_kidx[half_i:], g_kidx[:half_i]])
    ones_i = jnp.ones((IDX_DIM,), F32)
    q_main = jnp.concatenate([g_q, ones_i, pad])
    q_swap = jnp.concatenate([swap_q, ones_i, pad])
    k_main = jnp.concatenate([g_k, g_kidx, pad])
    k_swap = jnp.concatenate([swap_k, swap_i, pad])
    gm = jnp.concatenate([jnp.tile(q_main, N_HEADS), k_main])
    gs = jnp.concatenate([jnp.tile(q_swap, N_HEADS), k_swap])
    return freq[None], sgn[None], gm[None], gs[None]


def kernel(x, positions, ev_g_norm, ev_w_in, ev_g_q, ev_g_k, ev_g_kidx, ev_g_vb, ev_w_s, ev_b_s,
           ev_w_out, od_g_norm, od_w_pw1, od_w_dw, od_b_dw, od_g_conv, od_w_pw2, ff_g_norm,
           ff_w1, ff_w2):
    B, S, D = x.shape
    assert D == D_MODEL and S % (2 * KEY_CHUNK) == 0
    tm = 512

    main_idx, swap_idx = _column_plan()
    w_in = ev_w_in[0]
    wm = _take_cols(w_in, main_idx).astype(BF16)
    ws = _take_cols(w_in, swap_idx).astype(BF16)
    freq, sgn, gm, gs = _rope_lane_tables(ev_g_q[0], ev_g_k[0], ev_g_kidx[0])
    vws = jnp.concatenate([jnp.ones((HEAD_DIM,), F32), jnp.full((N_HEADS,), IDX_SCALE, F32),
                           jnp.zeros((LANES - HEAD_DIM - N_HEADS,), F32)])[None]
    bsp = jnp.repeat(ev_b_s[0].T, GROUP_DIM, axis=1)
    pos = positions.astype(F32)[..., None]

    r, kk, vt, wt, b_out = _in_stage(
        x, pos, ev_g_norm[0][None], wm, ws, freq, sgn, gm, gs, vws, ev_g_vb[0][None],
        ev_w_s[0], bsp, tm)
    a_out = _dsa(r, kk.reshape(B, S // KEY_CHUNK, KEY_CHUNK, LANES), vt, wt)
    x1 = _out_ffn(x.reshape(B * S, D), a_out.reshape(B * S, WIDTH_A), b_out.reshape(B * S, WIDTH_B),
                  ev_w_out[0].astype(BF16), ff_g_norm[0][None], ff_w1[0].astype(BF16),
                  ff_w2[0].astype(BF16), tm)

    x2 = _conv_ffn(x1.reshape(B, S, D), od_g_norm[0][None], od_w_pw1[0].astype(BF16), od_w_dw[0],
                   od_b_dw[0][None], od_g_conv[0][None], od_w_pw2[0].astype(BF16),
                   ff_g_norm[1][None], ff_w1[1].astype(BF16), ff_w2[1].astype(BF16), tm)
    return x2
```

```python
import functools

import numpy as np
import jax
import jax.numpy as jnp
from jax import lax
from jax.experimental import pallas as pl
from jax.experimental.pallas import tpu as pltpu

F32 = jnp.float32
BF16 = jnp.bfloat16

D_MODEL = 1024
N_HEADS = 8
HEAD_DIM = 64
IDX_DIM = 32
TOPK = 256
Q_BLOCK = 128
N_GROUPS = 8
GROUP_DIM = 64
WIDTH_A = N_HEADS * HEAD_DIM
WIDTH_B = N_GROUPS * GROUP_DIM
CHUNK = 128
CONV_WIDTH = 31
D_FF = 4 * D_MODEL
ROPE_THETA = 10000.0
EPS = 1e-6
IDX_SCALE = (N_HEADS ** -0.5) * (IDX_DIM ** -0.5)

LANES = 128
SUBLANES = 8
KEY_CHUNK = 512
LOG2_E = 1.4426950408889634
COUNT_ROWS = 64
V_ROWS = HEAD_DIM + 16
ROPE_COLS = (N_HEADS + 1) * LANES
MAIN_COLS = ROPE_COLS + LANES + 2 * WIDTH_B
CONV_HALO = 32
VMEM_LIMIT = 56 * 1024 * 1024


def _rms_rows(x, g):
    ms = jnp.mean(x * x, axis=-1, keepdims=True)
    return x * lax.rsqrt(ms + EPS) * g


def _sigmoid(x):
    return 1.0 / (1.0 + jnp.exp(-x))


def _gelu_tanh(x):
    c = np.float32(np.sqrt(2.0 / np.pi))
    return 0.5 * x * (1.0 + jnp.tanh(c * (x + 0.044715 * (x * x * x))))


def _ffn(x, g, w1_ref, w2_ref):
    h = _rms_rows(x, g).astype(BF16)
    acc = x
    slab = D_FF // 4
    for c in range(4):
        t = jnp.dot(h, w1_ref[:, c * slab:(c + 1) * slab], preferred_element_type=F32)
        t = jnp.maximum(t, 0.0)
        t = (t * t).astype(BF16)
        acc = acc + jnp.dot(t, w2_ref[c * slab:(c + 1) * slab, :], preferred_element_type=F32)
    return acc


def _in_stage_kernel(x_ref, pos_ref, gn_ref, wm_ref, ws_ref, freq_ref, sgn_ref,
                     gm_ref, gs_ref, vws_ref, gvb_ref, wsp_ref, bsp_ref,
                     r_ref, kk_ref, vt_ref, wt_ref, b_ref):
    tm = x_ref.shape[1]
    x = x_ref[0]
    h = _rms_rows(x, gn_ref[...]).astype(BF16)
    z = jnp.dot(h, wm_ref[...], preferred_element_type=F32)
    zs = jnp.dot(h, ws_ref[...], preferred_element_type=F32)

    lane = lax.broadcasted_iota(jnp.int32, (tm, LANES), 1)
    ang = pos_ref[0] * freq_ref[...]
    cos = jnp.cos(ang)
    sin = jnp.sin(ang) * sgn_ref[...]
    is_main = lane < HEAD_DIM

    for blk in range(N_HEADS + 1):
        sl = slice(blk * LANES, (blk + 1) * LANES)
        zb = z[:, sl]
        zw = zs[:, sl]
        sq = zb * zb
        ms_a = jnp.sum(jnp.where(is_main, sq, 0.0), axis=-1, keepdims=True) * (1.0 / HEAD_DIM)
        if blk < N_HEADS:
            inv = jnp.where(is_main, lax.rsqrt(ms_a + EPS), 1.0)
        else:
            ms_b = jnp.sum(jnp.where(is_main, 0.0, sq), axis=-1, keepdims=True) * (1.0 / IDX_DIM)
            inv = jnp.where(is_main, lax.rsqrt(ms_a + EPS), lax.rsqrt(ms_b + EPS))
        y = (zb * gm_ref[:, sl]) * inv
        yw = (zw * gs_ref[:, sl]) * inv
        out = (y * cos + yw * sin).astype(BF16)
        if blk < N_HEADS:
            r_ref[0, :, sl] = out
        else:
            kk_ref[0] = out

    vw = z[:, ROPE_COLS:ROPE_COLS + LANES] * vws_ref[...]
    vwt = vw.T
    for j in range(tm // KEY_CHUNK):
        vt_ref[0, j] = jnp.concatenate(
            [vwt[0:HEAD_DIM, j * KEY_CHUNK:(j + 1) * KEY_CHUNK],
             jnp.ones((V_ROWS - HEAD_DIM, KEY_CHUNK), F32)], axis=0).astype(BF16)
    wt_ref[0] = vwt[HEAD_DIM:HEAD_DIM + N_HEADS, :]

    u0 = ROPE_COLS + LANES
    ug = _gelu_tanh(z[:, u0:u0 + WIDTH_B])
    vg = _gelu_tanh(z[:, u0 + WIDTH_B:u0 + 2 * WIDTH_B])
    vn = _rms_rows(vg, gvb_ref[...]).astype(BF16)
    row = lax.broadcasted_iota(jnp.int32, (CHUNK, CHUNK), 0)
    col = lax.broadcasted_iota(jnp.int32, (CHUNK, CHUNK), 1)
    tril = col <= row
    wmix = [jnp.where(tril, wsp_ref[g], 0.0).astype(BF16) for g in range(N_GROUPS)]
    low = lax.broadcasted_iota(jnp.int32, (CHUNK, LANES), 1) < GROUP_DIM
    for c in range(tm // CHUNK):
        rows = slice(c * CHUNK, (c + 1) * CHUNK)
        for p in range(N_GROUPS // 2):
            cols = slice(p * LANES, (p + 1) * LANES)
            vb = vn[rows, cols]
            s0 = jnp.dot(wmix[2 * p], vb, preferred_element_type=F32)
            s1 = jnp.dot(wmix[2 * p + 1], vb, preferred_element_type=F32)
            sv = jnp.where(low, s0, s1) + bsp_ref[:, cols]
            b_ref[0, rows, cols] = (ug[rows, cols] * sv).astype(BF16)


def _in_stage(x, pos, gn, wm, ws, freq, sgn, gm, gs, vws, gvb, wsp, bsp, tm):
    B, S, D = x.shape
    n_kc = S // KEY_CHUNK
    const = lambda *shape: pl.BlockSpec(shape, lambda b, i: (0,) * len(shape),
                                        pipeline_mode=pl.Buffered(1))
    return pl.pallas_call(
        _in_stage_kernel,
        grid=(B, S // tm),
        in_specs=[
            pl.BlockSpec((1, tm, D), lambda b, i: (b, i, 0)),
            pl.BlockSpec((1, tm, 1), lambda b, i: (b, i, 0)),
            const(1, D),
            const(D, MAIN_COLS),
            const(D, ROPE_COLS),
            const(1, LANES),
            const(1, LANES),
            const(1, ROPE_COLS),
            const(1, ROPE_COLS),
            const(1, LANES),
            const(1, WIDTH_B),
            const(N_GROUPS, CHUNK, CHUNK),
            const(CHUNK, WIDTH_B),
        ],
        out_specs=[
            pl.BlockSpec((1, tm, N_HEADS * LANES), lambda b, i: (b, i, 0)),
            pl.BlockSpec((1, tm, LANES), lambda b, i: (b, i, 0)),
            pl.BlockSpec((1, tm // KEY_CHUNK, V_ROWS, KEY_CHUNK), lambda b, i: (b, i, 0, 0)),
            pl.BlockSpec((1, N_HEADS, tm), lambda b, i: (b, 0, i)),
            pl.BlockSpec((1, tm, WIDTH_B), lambda b, i: (b, i, 0)),
        ],
        out_shape=[
            jax.ShapeDtypeStruct((B, S, N_HEADS * LANES), BF16),
            jax.ShapeDtypeStruct((B, S, LANES), BF16),
            jax.ShapeDtypeStruct((B, n_kc, V_ROWS, KEY_CHUNK), BF16),
            jax.ShapeDtypeStruct((B, N_HEADS, S), F32),
            jax.ShapeDtypeStruct((B, S, WIDTH_B), BF16),
        ],
        compiler_params=pltpu.CompilerParams(
            dimension_semantics=("parallel", "parallel"), vmem_limit_bytes=VMEM_LIMIT),
        name="even_in_stage",
    )(x, pos, gn, wm, ws, freq, sgn, gm, gs, vws, gvb, wsp, bsp)


def _key_to_float(u):
    key = u ^ jnp.int32(-2 ** 31)
    bits = jnp.where(key >= 0, key, key ^ jnp.int32(0x7FFFFFFF))
    return lax.bitcast_convert_type(bits, jnp.float32)


def _dsa_kernel(r_ref, kk_ref, vt_ref, wt_ref, o_ref, sc_ref, buf_a, buf_b, bias_ref, p_ref):
    qb = pl.program_id(1)
    n_ch = (qb * Q_BLOCK + Q_BLOCK + KEY_CHUNK - 1) // KEY_CHUNK
    n_pair = (n_ch + 1) // 2

    r = r_ref[0].astype(F32)
    lane = lax.broadcasted_iota(jnp.int32, (Q_BLOCK, LANES), 1)
    is_q = lane < HEAD_DIM
    q_scale = (HEAD_DIM ** -0.5) * LOG2_E
    rq = jnp.concatenate(
        [jnp.where(is_q, r[:, h * LANES:(h + 1) * LANES] * q_scale, 0.0)
         for h in range(N_HEADS)], axis=0).astype(BF16)
    rqi = jnp.concatenate(
        [jnp.where(is_q, 0.0, r[:, h * LANES:(h + 1) * LANES]) for h in range(N_HEADS)],
        axis=0).astype(BF16)
    wt = wt_ref[0]
    t_idx = qb * Q_BLOCK + lax.broadcasted_iota(jnp.int32, (1, LANES), 1)
    row_iota = lax.broadcasted_iota(jnp.int32, (KEY_CHUNK, LANES), 0)
    nt = (((1,), (1,)), ((), ()))

    last = kk_ref.shape[1] - 1

    def keys_dot(c, w):
        return lax.dot_general(kk_ref[0, jnp.minimum(c, last)], w, nt, preferred_element_type=F32)


    def write_scores(c, lg_ref):
        sc = wt[0:1, :] * jnp.maximum(lg_ref[:, 0:LANES], 0.0)
        for h in range(1, N_HEADS):
            sc = sc + wt[h:h + 1, :] * jnp.maximum(lg_ref[:, h * LANES:(h + 1) * LANES], 0.0)
        s_idx = c * KEY_CHUNK + row_iota
        sc_ref[c] = jnp.where(s_idx <= t_idx, sc, -jnp.inf)

    def score_pair(j, carry):
        buf_b[...] = keys_dot(2 * j + 1, rqi)
        write_scores(2 * j, buf_a)
        buf_a[...] = keys_dot(2 * j + 2, rqi)
        write_scores(2 * j + 1, buf_b)
        return carry

    buf_a[...] = keys_dot(0, rqi)
    lax.fori_loop(0, n_pair, score_pair, 0)

    k_row = jnp.minimum(t_idx + 1, TOPK).astype(F32)

    slabs = [slice(s * COUNT_ROWS, (s + 1) * COUNT_ROWS) for s in range(KEY_CHUNK // COUNT_ROWS)]

    def count_rows(pred):
        def body(j, acc):
            for half in range(2):
                for rows in slabs:
                    acc = acc + jnp.where(pred(sc_ref[2 * j + half, rows, :]), 1.0, 0.0)
            return acc
        acc = lax.fori_loop(0, n_pair, body, jnp.zeros((COUNT_ROWS, LANES), F32))
        return acc.sum(axis=0, keepdims=True)

    def bit_body(i, carry):
        u, cnt_u = carry
        trial = u | lax.shift_left(jnp.int32(1), (31 - i).astype(jnp.int32))
        cvec = _key_to_float(trial)
        cnt = count_rows(lambda x: x >= cvec)
        take = cnt >= k_row
        return jnp.where(take, trial, u), jnp.where(take, cnt, cnt_u)

    u, cnt_thr = lax.fori_loop(
        0, 32, bit_body, (jnp.zeros((1, LANES), jnp.int32), jnp.zeros((1, LANES), F32)))
    thr = _key_to_float(u)

    @pl.when(jnp.max(cnt_thr - k_row) > 0.0)
    def _():
        n_tie = count_rows(lambda x: x == thr)
        need = k_row - (cnt_thr - n_tie)
        ri = lax.broadcasted_iota(jnp.int32, (KEY_CHUNK, KEY_CHUNK), 0)
        ci = lax.broadcasted_iota(jnp.int32, (KEY_CHUNK, KEY_CHUNK), 1)
        before = jnp.where(ci < ri, 1.0, 0.0).astype(BF16)

        def drop_body(c, seen):
            x = sc_ref[c]
            tie = x == thr
            tie_f = jnp.where(tie, 1.0, 0.0)
            rank = seen + jnp.dot(before, tie_f.astype(BF16), preferred_element_type=F32)
            sc_ref[c] = jnp.where(tie, jnp.where(rank >= need, -jnp.inf, x), x)
            return seen + jnp.sum(tie_f, axis=0, keepdims=True)

        lax.fori_loop(0, n_ch, drop_body, jnp.zeros((1, LANES), F32))

    def attend(c, st_ref, carry):
        ms, accs = carry
        vt = vt_ref[0, c]
        for rows in slabs:
            bias_ref[rows, :] = jnp.where(sc_ref[c, rows, :] >= thr, 0.0, -jnp.inf)
        new_ms, new_accs = [], []
        for pr in range(N_HEADS // 2):
            heads = (2 * pr, 2 * pr + 1)
            cols = [slice(h * LANES, (h + 1) * LANES) for h in heads]
            mx = [None, None]
            for rows in slabs:
                bias = bias_ref[rows, :]
                for i in range(2):
                    v = st_ref[rows, cols[i]] + bias
                    st_ref[rows, cols[i]] = v
                    mx[i] = v if mx[i] is None else jnp.maximum(mx[i], v)
            m_new = [jnp.maximum(ms[h], jnp.max(mx[i], axis=0, keepdims=True))
                     for i, h in enumerate(heads)]
            for rows in slabs:
                for i in range(2):
                    p_ref[rows, cols[i]] = jnp.exp2(st_ref[rows, cols[i]] - m_new[i]).astype(BF16)
            alpha = jnp.concatenate([jnp.exp2(ms[h] - m_new[i]) for i, h in enumerate(heads)], axis=1)
            pv = jnp.dot(vt, p_ref[:, 2 * pr * LANES:(2 * pr + 2) * LANES],
                         preferred_element_type=F32)
            new_accs.append(alpha * accs[pr] + pv)
            new_ms.extend(m_new)
        return tuple(new_ms), tuple(new_accs)

    def attn_pair(j, carry):
        buf_b[...] = keys_dot(2 * j + 1, rq)
        carry = attend(2 * j, buf_a, carry)
        buf_a[...] = keys_dot(2 * j + 2, rq)
        return attend(2 * j + 1, buf_b, carry)

    ms0 = tuple(jnp.full((1, LANES), -1e30, F32) for _ in range(N_HEADS))
    accs0 = tuple(jnp.zeros((V_ROWS, 2 * LANES), F32) for _ in range(N_HEADS // 2))
    buf_a[...] = keys_dot(0, rq)
    _, accs = lax.fori_loop(0, n_pair, attn_pair, (ms0, accs0))
    outs = []
    for h in range(N_HEADS):
        a = accs[h // 2][:, (h % 2) * LANES:(h % 2 + 1) * LANES]
        outs.append(a[0:HEAD_DIM] * (1.0 / a[HEAD_DIM:HEAD_DIM + 1]))
    o_ref[0] = jnp.concatenate(outs, axis=0).T.astype(BF16)


def _dsa(r, kk, vt, wt):
    B, S, _ = r.shape
    n_kc = S // KEY_CHUNK
    return pl.pallas_call(
        _dsa_kernel,
        grid=(B, S // Q_BLOCK),
        in_specs=[
            pl.BlockSpec((1, Q_BLOCK, N_HEADS * LANES), lambda b, i: (b, i, 0)),
            pl.BlockSpec((1, n_kc, KEY_CHUNK, LANES), lambda b, i: (b, 0, 0, 0)),
            pl.BlockSpec((1, n_kc, V_ROWS, KEY_CHUNK), lambda b, i: (b, 0, 0, 0)),
            pl.BlockSpec((1, N_HEADS, Q_BLOCK), lambda b, i: (b, 0, i)),
        ],
        out_specs=pl.BlockSpec((1, Q_BLOCK, WIDTH_A), lambda b, i: (b, i, 0)),
        out_shape=jax.ShapeDtypeStruct((B, S, WIDTH_A), BF16),
        scratch_shapes=[pltpu.VMEM((n_kc + 1, KEY_CHUNK, LANES), F32),
                        pltpu.VMEM((KEY_CHUNK, N_HEADS * LANES), F32),
                        pltpu.VMEM((KEY_CHUNK, N_HEADS * LANES), F32),
                        pltpu.VMEM((KEY_CHUNK, LANES), F32),
                        pltpu.VMEM((KEY_CHUNK, N_HEADS * LANES), BF16)],
        compiler_params=pltpu.CompilerParams(
            dimension_semantics=("parallel", "parallel"), vmem_limit_bytes=VMEM_LIMIT),
        name="dsa_attention",
    )(r, kk, vt, wt)


def _out_ffn_kernel(x_ref, a_ref, b_ref, wo_ref, g_ref, w1_ref, w2_ref, o_ref):
    ab = jnp.concatenate([a_ref[...], b_ref[...]], axis=1)
    o_ref[...] = x_ref[...] + jnp.dot(ab, wo_ref[...], preferred_element_type=F32)
    o_ref[...] = _ffn(o_ref[...], g_ref[...], w1_ref, w2_ref)


def _out_ffn(x, a, b, wo, g, w1, w2, tm):
    T, D = x.shape
    const = lambda *shape: pl.BlockSpec(shape, lambda i: (0,) * len(shape),
                                        pipeline_mode=pl.Buffered(1))
    return pl.pallas_call(
        _out_ffn_kernel,
        grid=(T // tm,),
        in_specs=[
            pl.BlockSpec((tm, D), lambda i: (i, 0)),
            pl.BlockSpec((tm, WIDTH_A), lambda i: (i, 0)),
            pl.BlockSpec((tm, WIDTH_B), lambda i: (i, 0)),
            const(WIDTH_A + WIDTH_B, D),
            const(1, D),
            const(D, D_FF),
            const(D_FF, D),
        ],
        out_specs=pl.BlockSpec((tm, D), lambda i: (i, 0)),
        out_shape=jax.ShapeDtypeStruct((T, D), F32),
        compiler_params=pltpu.CompilerParams(
            dimension_semantics=("parallel",), vmem_limit_bytes=VMEM_LIMIT),
        name="out_proj_ffn",
    )(x, a, b, wo, g, w1, w2)


def _conv_ffn_kernel(x_ref, gn_ref, wp1_ref, wdw_ref, bdw_ref, gc_ref, wp2_ref,
                     g_ref, w1_ref, w2_ref, o_ref, ybuf):
    tm = x_ref.shape[1]
    x = x_ref[0]
    h = _rms_rows(x, gn_ref[...]).astype(BF16)
    z = jnp.dot(h, wp1_ref[...], preferred_element_type=F32)
    y = z[:, :D_MODEL] * _sigmoid(z[:, D_MODEL:])

    @pl.when(pl.program_id(1) == 0)
    def _():
        ybuf[0:CONV_HALO, :] = jnp.zeros((CONV_HALO, D_MODEL), F32)
        ybuf[CONV_HALO + tm:, :] = jnp.zeros((SUBLANES, D_MODEL), F32)

    ybuf[CONV_HALO:CONV_HALO + tm, :] = y
    first = CONV_HALO - (CONV_WIDTH - 1)
    span = tm + SUBLANES
    acc = None
    for res in reversed(range(SUBLANES)):
        part = None
        for j in range(CONV_WIDTH):
            if (first + j) % SUBLANES == res:
                base = first + j - res
                term = wdw_ref[j:j + 1, :] * ybuf[base:base + span, :]
                part = term if part is None else part + term
        acc = part if acc is None else part + pltpu.roll(acc, span - 1, axis=0)
    acc = acc[0:tm, :] + bdw_ref[...]
    ybuf[0:CONV_HALO, :] = ybuf[tm:tm + CONV_HALO, :]

    c = _rms_rows(acc, gc_ref[...])
    c = (c * _sigmoid(c)).astype(BF16)
    o_ref[0] = x + jnp.dot(c, wp2_ref[...], preferred_element_type=F32)
    o_ref[0] = _ffn(o_ref[0], g_ref[...], w1_ref, w2_ref)


def _conv_ffn(x, gn, wp1, wdw, bdw, gc, wp2, g, w1, w2, tm):
    B, S, D = x.shape
    const = lambda *shape: pl.BlockSpec(shape, lambda b, i: (0,) * len(shape),
                                        pipeline_mode=pl.Buffered(1))
    return pl.pallas_call(
        _conv_ffn_kernel,
        grid=(B, S // tm),
        in_specs=[
            pl.BlockSpec((1, tm, D), lambda b, i: (b, i, 0)),
            const(1, D),
            const(D, 2 * D),
            const(CONV_WIDTH, D),
            const(1, D),
            const(1, D),
            const(D, D),
            const(1, D),
            const(D, D_FF),
            const(D_FF, D),
        ],
        out_specs=pl.BlockSpec((1, tm, D), lambda b, i: (b, i, 0)),
        out_shape=jax.ShapeDtypeStruct((B, S, D), F32),
        scratch_shapes=[pltpu.VMEM((tm + CONV_HALO + SUBLANES, D), F32)],
        compiler_params=pltpu.CompilerParams(
            dimension_semantics=("arbitrary", "arbitrary"), vmem_limit_bytes=VMEM_LIMIT),
        name="conv_module_ffn",
    )(x, gn, wp1, wdw, bdw, gc, wp2, g, w1, w2)


def _column_plan():
    q0, k0, v0 = 0, WIDTH_A, WIDTH_A + HEAD_DIM
    qi0 = v0 + HEAD_DIM
    ki0 = qi0 + N_HEADS * IDX_DIM
    wi0 = ki0 + IDX_DIM
    u0 = wi0 + N_HEADS
    vb0 = u0 + WIDTH_B
    main = -np.ones((MAIN_COLS,), np.int64)
    swap = -np.ones((ROPE_COLS,), np.int64)

    def put(dst, src, width):
        half = width // 2
        for d in range(width):
            main[dst + d] = src + d
            swap[dst + d] = src + (d + half) % width

    for h in range(N_HEADS):
        put(h * LANES, q0 + h * HEAD_DIM, HEAD_DIM)
        put(h * LANES + HEAD_DIM, qi0 + h * IDX_DIM, IDX_DIM)
    put(N_HEADS * LANES, k0, HEAD_DIM)
    put(N_HEADS * LANES + HEAD_DIM, ki0, IDX_DIM)
    main[ROPE_COLS:ROPE_COLS + HEAD_DIM] = v0 + np.arange(HEAD_DIM)
    main[ROPE_COLS + HEAD_DIM:ROPE_COLS + HEAD_DIM + N_HEADS] = wi0 + np.arange(N_HEADS)
    main[ROPE_COLS + LANES:ROPE_COLS + LANES + WIDTH_B] = u0 + np.arange(WIDTH_B)
    main[ROPE_COLS + LANES + WIDTH_B:] = vb0 + np.arange(WIDTH_B)
    return main, swap


def _take_cols(w, idx):
    wz = jnp.concatenate([w, jnp.zeros((w.shape[0], 1), w.dtype)], axis=1)
    return jnp.take(wz, jnp.asarray(np.where(idx < 0, w.shape[1], idx)), axis=1)


def _rope_lane_tables(g_q, g_k, g_kidx):
    half_q, half_i = HEAD_DIM // 2, IDX_DIM // 2
    f_q = ROPE_THETA ** (-jnp.arange(half_q, dtype=F32) * 2.0 / HEAD_DIM)
    f_i = ROPE_THETA ** (-jnp.arange(half_i, dtype=F32) * 2.0 / IDX_DIM)
    pad = jnp.zeros((LANES - HEAD_DIM - IDX_DIM,), F32)
    freq = jnp.concatenate([f_q, f_q, f_i, f_i, pad])
    sgn = jnp.concatenate([-jnp.ones((half_q,), F32), jnp.ones((half_q,), F32),
                           -jnp.ones((half_i,), F32), jnp.ones((half_i,), F32), pad])
    swap_q = jnp.concatenate([g_q[half_q:], g_q[:half_q]])
    swap_k = jnp.concatenate([g_k[half_q:], g_k[:half_q]])
    swap_i = jnp.concatenate([g_kidx[half_i:], g_kidx[:half_i]])
    ones_i = jnp.ones((IDX_DIM,), F32)
    q_main = jnp.concatenate([g_q, ones_i, pad])
    q_swap = jnp.concatenate([swap_q, ones_i, pad])
    k_main = jnp.concatenate([g_k, g_kidx, pad])
    k_swap = jnp.concatenate([swap_k, swap_i, pad])
    gm = jnp.concatenate([jnp.tile(q_main, N_HEADS), k_main])
    gs = jnp.concatenate([jnp.tile(q_swap, N_HEADS), k_swap])
    return freq[None], sgn[None], gm[None], gs[None]


def kernel(x, positions, ev_g_norm, ev_w_in, ev_g_q, ev_g_k, ev_g_kidx, ev_g_vb, ev_w_s, ev_b_s,
           ev_w_out, od_g_norm, od_w_pw1, od_w_dw, od_b_dw, od_g_conv, od_w_pw2, ff_g_norm,
           ff_w1, ff_w2):
    B, S, D = x.shape
    assert D == D_MODEL and S % (2 * KEY_CHUNK) == 0
    tm = 512

    main_idx, swap_idx = _column_plan()
    w_in = ev_w_in[0]
    wm = _take_cols(w_in, main_idx).astype(BF16)
    ws = _take_cols(w_in, swap_idx).astype(BF16)
    freq, sgn, gm, gs = _rope_lane_tables(ev_g_q[0], ev_g_k[0], ev_g_kidx[0])
    vws = jnp.concatenate([jnp.ones((HEAD_DIM,), F32), jnp.full((N_HEADS,), IDX_SCALE, F32),
                           jnp.zeros((LANES - HEAD_DIM - N_HEADS,), F32)])[None]
    bsp = jnp.repeat(ev_b_s[0].T, GROUP_DIM, axis=1)
    pos = positions.astype(F32)[..., None]

    r, kk, vt, wt, b_out = _in_stage(
        x, pos, ev_g_norm[0][None], wm, ws, freq, sgn, gm, gs, vws, ev_g_vb[0][None],
        ev_w_s[0], bsp, tm)
    a_out = _dsa(r, kk.reshape(B, S // KEY_CHUNK, KEY_CHUNK, LANES), vt, wt)
    x1 = _out_ffn(x.reshape(B * S, D), a_out.reshape(B * S, WIDTH_A), b_out.reshape(B * S, WIDTH_B),
                  ev_w_out[0].astype(BF16), ff_g_norm[0][None], ff_w1[0].astype(BF16),
                  ff_w2[0].astype(BF16), tm)

    x2 = _conv_ffn(x1.reshape(B, S, D), od_g_norm[0][None], od_w_pw1[0].astype(BF16), od_w_dw[0],
                   od_b_dw[0][None], od_g_conv[0][None], od_w_pw2[0].astype(BF16),
                   ff_g_norm[1][None], ff_w1[1].astype(BF16), ff_w2[1].astype(BF16), tm)
    return x2
```

```python
import functools

import numpy as np
import jax
import jax.numpy as jnp
from jax import lax
from jax.experimental import pallas as pl
from jax.experimental.pallas import tpu as pltpu

F32 = jnp.float32
BF16 = jnp.bfloat16

D_MODEL = 1024
N_HEADS = 8
HEAD_DIM = 64
IDX_DIM = 32
TOPK = 256
Q_BLOCK = 128
N_GROUPS = 8
GROUP_DIM = 64
WIDTH_A = N_HEADS * HEAD_DIM
WIDTH_B = N_GROUPS * GROUP_DIM
CHUNK = 128
CONV_WIDTH = 31
D_FF = 4 * D_MODEL
ROPE_THETA = 10000.0
EPS = 1e-6
IDX_SCALE = (N_HEADS ** -0.5) * (IDX_DIM ** -0.5)

LANES = 128
SUBLANES = 8
KEY_CHUNK = 512
LOG2_E = 1.4426950408889634
COUNT_ROWS = 64
V_ROWS = HEAD_DIM + 16
ROPE_COLS = (N_HEADS + 1) * LANES
MAIN_COLS = ROPE_COLS + LANES + 2 * WIDTH_B
CONV_HALO = 32
VMEM_LIMIT = 56 * 1024 * 1024


def _rms_rows(x, g):
    ms = jnp.mean(x * x, axis=-1, keepdims=True)
    return x * lax.rsqrt(ms + EPS) * g


def _sigmoid(x):
    return 1.0 / (1.0 + jnp.exp(-x))


def _gelu_tanh(x):
    c = np.float32(np.sqrt(2.0 / np.pi))
    return 0.5 * x * (1.0 + jnp.tanh(c * (x + 0.044715 * (x * x * x))))


def _ffn(x, g, w1_ref, w2_ref):
    h = _rms_rows(x, g).astype(BF16)
    acc = x
    slab = D_FF // 4
    for c in range(4):
        t = jnp.dot(h, w1_ref[:, c * slab:(c + 1) * slab], preferred_element_type=F32)
        t = jnp.maximum(t, 0.0)
        t = (t * t).astype(BF16)
        acc = acc + jnp.dot(t, w2_ref[c * slab:(c + 1) * slab, :], preferred_element_type=F32)
    return acc


def _in_stage_kernel(x_ref, pos_ref, gn_ref, wm_ref, ws_ref, freq_ref, sgn_ref,
                     gm_ref, gs_ref, vws_ref, gvb_ref, wsp_ref, bsp_ref,
                     r_ref, kk_ref, vt_ref, wt_ref, b_ref):
    tm = x_ref.shape[1]
    x = x_ref[0]
    h = _rms_rows(x, gn_ref[...]).astype(BF16)
    z = jnp.dot(h, wm_ref[...], preferred_element_type=F32)
    zs = jnp.dot(h, ws_ref[...], preferred_element_type=F32)

    lane = lax.broadcasted_iota(jnp.int32, (tm, LANES), 1)
    ang = pos_ref[0] * freq_ref[...]
    cos = jnp.cos(ang)
    sin = jnp.sin(ang) * sgn_ref[...]
    is_main = lane < HEAD_DIM

    for blk in range(N_HEADS + 1):
        sl = slice(blk * LANES, (blk + 1) * LANES)
        zb = z[:, sl]
        zw = zs[:, sl]
        sq = zb * zb
        ms_a = jnp.sum(jnp.where(is_main, sq, 0.0), axis=-1, keepdims=True) * (1.0 / HEAD_DIM)
        if blk < N_HEADS:
            inv = jnp.where(is_main, lax.rsqrt(ms_a + EPS), 1.0)
        else:
            ms_b = jnp.sum(jnp.where(is_main, 0.0, sq), axis=-1, keepdims=True) * (1.0 / IDX_DIM)
            inv = jnp.where(is_main, lax.rsqrt(ms_a + EPS), lax.rsqrt(ms_b + EPS))
        y = (zb * gm_ref[:, sl]) * inv
        yw = (zw * gs_ref[:, sl]) * inv
        out = (y * cos + yw * sin).astype(BF16)
        if blk < N_HEADS:
            r_ref[0, :, sl] = out
        else:
            kk_ref[0] = out

    vw = z[:, ROPE_COLS:ROPE_COLS + LANES] * vws_ref[...]
    vwt = vw.T
    for j in range(tm // KEY_CHUNK):
        vt_ref[0, j] = jnp.concatenate(
            [vwt[0:HEAD_DIM, j * KEY_CHUNK:(j + 1) * KEY_CHUNK],
             jnp.ones((V_ROWS - HEAD_DIM, KEY_CHUNK), F32)], axis=0).astype(BF16)
    wt_ref[0] = vwt[HEAD_DIM:HEAD_DIM + N_HEADS, :]

    u0 = ROPE_COLS + LANES
    ug = _gelu_tanh(z[:, u0:u0 + WIDTH_B])
    vg = _gelu_tanh(z[:, u0 + WIDTH_B:u0 + 2 * WIDTH_B])
    vn = _rms_rows(vg, gvb_ref[...]).astype(BF16)
    row = lax.broadcasted_iota(jnp.int32, (CHUNK, CHUNK), 0)
    col = lax.broadcasted_iota(jnp.int32, (CHUNK, CHUNK), 1)
    tril = col <= row
    wmix = [jnp.where(tril, wsp_ref[g], 0.0).astype(BF16) for g in range(N_GROUPS)]
    low = lax.broadcasted_iota(jnp.int32, (CHUNK, LANES), 1) < GROUP_DIM
    for c in range(tm // CHUNK):
        rows = slice(c * CHUNK, (c + 1) * CHUNK)
        for p in range(N_GROUPS // 2):
            cols = slice(p * LANES, (p + 1) * LANES)
            vb = vn[rows, cols]
            s0 = jnp.dot(wmix[2 * p], vb, preferred_element_type=F32)
            s1 = jnp.dot(wmix[2 * p + 1], vb, preferred_element_type=F32)
            sv = jnp.where(low, s0, s1) + bsp_ref[:, cols]
            b_ref[0, rows, cols] = (ug[rows, cols] * sv).astype(BF16)


def _in_stage(x, pos, gn, wm, ws, freq, sgn, gm, gs, vws, gvb, wsp, bsp, tm):
    B, S, D = x.shape
    n_kc = S // KEY_CHUNK
    const = lambda *shape: pl.BlockSpec(shape, lambda b, i: (0,) * len(shape),
                                        pipeline_mode=pl.Buffered(1))
    return pl.pallas_call(
        _in_stage_kernel,
        grid=(B, S // tm),
        in_specs=[
            pl.BlockSpec((1, tm, D), lambda b, i: (b, i, 0)),
            pl.BlockSpec((1, tm, 1), lambda b, i: (b, i, 0)),
            const(1, D),
            const(D, MAIN_COLS),
            const(D, ROPE_COLS),
            const(1, LANES),
            const(1, LANES),
            const(1, ROPE_COLS),
            const(1, ROPE_COLS),
            const(1, LANES),
            const(1, WIDTH_B),
            const(N_GROUPS, CHUNK, CHUNK),
            const(CHUNK, WIDTH_B),
        ],
        out_specs=[
            pl.BlockSpec((1, tm, N_HEADS * LANES), lambda b, i: (b, i, 0)),
            pl.BlockSpec((1, tm, LANES), lambda b, i: (b, i, 0)),
            pl.BlockSpec((1, tm // KEY_CHUNK, V_ROWS, KEY_CHUNK), lambda b, i: (b, i, 0, 0)),
            pl.BlockSpec((1, N_HEADS, tm), lambda b, i: (b, 0, i)),
            pl.BlockSpec((1, tm, WIDTH_B), lambda b, i: (b, i, 0)),
        ],
        out_shape=[
            jax.ShapeDtypeStruct((B, S, N_HEADS * LANES), BF16),
            jax.ShapeDtypeStruct((B, S, LANES), BF16),
            jax.ShapeDtypeStruct((B, n_kc, V_ROWS, KEY_CHUNK), BF16),
            jax.ShapeDtypeStruct((B, N_HEADS, S), F32),
            jax.ShapeDtypeStruct((B, S, WIDTH_B), BF16),
        ],
        compiler_params=pltpu.CompilerParams(
            dimension_semantics=("parallel", "parallel"), vmem_limit_bytes=VMEM_LIMIT),
        name="even_in_stage",
    )(x, pos, gn, wm, ws, freq, sgn, gm, gs, vws, gvb, wsp, bsp)


def _key_to_float(u):
    key = u ^ jnp.int32(-2 ** 31)
    bits = jnp.where(key >= 0, key, key ^ jnp.int32(0x7FFFFFFF))
    return lax.bitcast_convert_type(bits, jnp.float32)


def _dsa_kernel(r_ref, kk_ref, vt_ref, wt_ref, o_ref, sc_ref, buf_a, buf_b):
    qb = pl.program_id(1)
    n_ch = (qb * Q_BLOCK + Q_BLOCK + KEY_CHUNK - 1) // KEY_CHUNK
    n_pair = (n_ch + 1) // 2

    r = r_ref[0].astype(F32)
    lane = lax.broadcasted_iota(jnp.int32, (Q_BLOCK, LANES), 1)
    is_q = lane < HEAD_DIM
    q_scale = (HEAD_DIM ** -0.5) * LOG2_E
    rq = jnp.concatenate(
        [jnp.where(is_q, r[:, h * LANES:(h + 1) * LANES] * q_scale, 0.0)
         for h in range(N_HEADS)], axis=0).astype(BF16)
    rqi = jnp.concatenate(
        [jnp.where(is_q, 0.0, r[:, h * LANES:(h + 1) * LANES]) for h in range(N_HEADS)],
        axis=0).astype(BF16)
    wt = wt_ref[0]
    t_idx = qb * Q_BLOCK + lax.broadcasted_iota(jnp.int32, (1, LANES), 1)
    row_iota = lax.broadcasted_iota(jnp.int32, (KEY_CHUNK, LANES), 0)
    nt = (((1,), (1,)), ((), ()))

    last = kk_ref.shape[1] - 1

    def keys_dot(c, w):
        return lax.dot_general(kk_ref[0, jnp.minimum(c, last)], w, nt, preferred_element_type=F32)


    def write_scores(c, lg_ref):
        sc = wt[0:1, :] * jnp.maximum(lg_ref[:, 0:LANES], 0.0)
        for h in range(1, N_HEADS):
            sc = sc + wt[h:h + 1, :] * jnp.maximum(lg_ref[:, h * LANES:(h + 1) * LANES], 0.0)
        s_idx = c * KEY_CHUNK + row_iota
        sc_ref[c] = jnp.where(s_idx <= t_idx, sc, -jnp.inf)

    def score_pair(j, carry):
        buf_b[...] = keys_dot(2 * j + 1, rqi)
        write_scores(2 * j, buf_a)
        buf_a[...] = keys_dot(2 * j + 2, rqi)
        write_scores(2 * j + 1, buf_b)
        return carry

    buf_a[...] = keys_dot(0, rqi)
    lax.fori_loop(0, n_pair, score_pair, 0)

    k_row = jnp.minimum(t_idx + 1, TOPK).astype(F32)

    slabs = [slice(s * COUNT_ROWS, (s + 1) * COUNT_ROWS) for s in range(KEY_CHUNK // COUNT_ROWS)]

    def count_rows(pred):
        def body(j, acc):
            for half in range(2):
                for rows in slabs:
                    acc = acc + jnp.where(pred(sc_ref[2 * j + half, rows, :]), 1.0, 0.0)
            return acc
        acc = lax.fori_loop(0, n_pair, body, jnp.zeros((COUNT_ROWS, LANES), F32))
        return acc.sum(axis=0, keepdims=True)

    def bit_body(i, carry):
        u, cnt_u = carry
        trial = u | lax.shift_left(jnp.int32(1), (31 - i).astype(jnp.int32))
        cvec = _key_to_float(trial)
        cnt = count_rows(lambda x: x >= cvec)
        take = cnt >= k_row
        return jnp.where(take, trial, u), jnp.where(take, cnt, cnt_u)

    u, cnt_thr = lax.fori_loop(
        0, 32, bit_body, (jnp.zeros((1, LANES), jnp.int32), jnp.zeros((1, LANES), F32)))
    thr = _key_to_float(u)

    @pl.when(jnp.max(cnt_thr - k_row) > 0.0)
    def _():
        n_tie = count_rows(lambda x: x == thr)
        need = k_row - (cnt_thr - n_tie)
        ri = lax.broadcasted_iota(jnp.int32, (KEY_CHUNK, KEY_CHUNK), 0)
        ci = lax.broadcasted_iota(jnp.int32, (KEY_CHUNK, KEY_CHUNK), 1)
        before = jnp.where(ci < ri, 1.0, 0.0).astype(BF16)

        def drop_body(c, seen):
            x = sc_ref[c]
            tie = x == thr
            tie_f = jnp.where(tie, 1.0, 0.0)
            rank = seen + jnp.dot(before, tie_f.astype(BF16), preferred_element_type=F32)
            sc_ref[c] = jnp.where(tie, jnp.where(rank >= need, -jnp.inf, x), x)
            return seen + jnp.sum(tie_f, axis=0, keepdims=True)

        lax.fori_loop(0, n_ch, drop_body, jnp.zeros((1, LANES), F32))

    def attend(c, st_ref, carry):
        ms, accs = carry
        vt = vt_ref[0, c]
        sel = sc_ref[c] >= thr
        new_ms, new_accs = [], []
        for pr in range(N_HEADS // 2):
            ps, alphas = [], []
            for hh in range(2):
                h = 2 * pr + hh
                m_old = ms[h]
                s = jnp.where(sel, st_ref[:, h * LANES:(h + 1) * LANES], -jnp.inf)
                m_new = jnp.maximum(m_old, jnp.max(s, axis=0, keepdims=True))
                ps.append(jnp.exp2(s - m_new).astype(BF16))
                alphas.append(jnp.exp2(m_old - m_new))
                new_ms.append(m_new)
            pv = jnp.dot(vt, jnp.concatenate(ps, axis=1), preferred_element_type=F32)
            new_accs.append(jnp.concatenate(alphas, axis=1) * accs[pr] + pv)
        return tuple(new_ms), tuple(new_accs)

    def attn_pair(j, carry):
        buf_b[...] = keys_dot(2 * j + 1, rq)
        carry = attend(2 * j, buf_a, carry)
        buf_a[...] = keys_dot(2 * j + 2, rq)
        return attend(2 * j + 1, buf_b, carry)

    ms0 = tuple(jnp.full((1, LANES), -1e30, F32) for _ in range(N_HEADS))
    accs0 = tuple(jnp.zeros((V_ROWS, 2 * LANES), F32) for _ in range(N_HEADS // 2))
    buf_a[...] = keys_dot(0, rq)
    _, accs = lax.fori_loop(0, n_pair, attn_pair, (ms0, accs0))
    outs = []
    for h in range(N_HEADS):
        a = accs[h // 2][:, (h % 2) * LANES:(h % 2 + 1) * LANES]
        outs.append(a[0:HEAD_DIM] * (1.0 / a[HEAD_DIM:HEAD_DIM + 1]))
    o_ref[0] = jnp.concatenate(outs, axis=0).T.astype(BF16)


def _dsa(r, kk, vt, wt):
    B, S, _ = r.shape
    n_kc = S // KEY_CHUNK
    return pl.pallas_call(
        _dsa_kernel,
        grid=(B, S // Q_BLOCK),
        in_specs=[
            pl.BlockSpec((1, Q_BLOCK, N_HEADS * LANES), lambda b, i: (b, i, 0)),
            pl.BlockSpec((1, n_kc, KEY_CHUNK, LANES), lambda b, i: (b, 0, 0, 0)),
            pl.BlockSpec((1, n_kc, V_ROWS, KEY_CHUNK), lambda b, i: (b, 0, 0, 0)),
            pl.BlockSpec((1, N_HEADS, Q_BLOCK), lambda b, i: (b, 0, i)),
        ],
        out_specs=pl.BlockSpec((1, Q_BLOCK, WIDTH_A), lambda b, i: (b, i, 0)),
        out_shape=jax.ShapeDtypeStruct((B, S, WIDTH_A), BF16),
        scratch_shapes=[pltpu.VMEM((n_kc + 1, KEY_CHUNK, LANES), F32),
                        pltpu.VMEM((KEY_CHUNK, N_HEADS * LANES), F32),
                        pltpu.VMEM((KEY_CHUNK, N_HEADS * LANES), F32)],
        compiler_params=pltpu.CompilerParams(
            dimension_semantics=("parallel", "parallel"), vmem_limit_bytes=VMEM_LIMIT),
        name="dsa_attention",
    )(r, kk, vt, wt)


def _out_ffn_kernel(x_ref, a_ref, b_ref, wo_ref, g_ref, w1_ref, w2_ref, o_ref):
    ab = jnp.concatenate([a_ref[...], b_ref[...]], axis=1)
    o_ref[...] = x_ref[...] + jnp.dot(ab, wo_ref[...], preferred_element_type=F32)
    o_ref[...] = _ffn(o_ref[...], g_ref[...], w1_ref, w2_ref)


def _out_ffn(x, a, b, wo, g, w1, w2, tm):
    T, D = x.shape
    const = lambda *shape: pl.BlockSpec(shape, lambda i: (0,) * len(shape),
                                        pipeline_mode=pl.Buffered(1))
    return pl.pallas_call(
        _out_ffn_kernel,
        grid=(T // tm,),
        in_specs=[
            pl.BlockSpec((tm, D), lambda i: (i, 0)),
            pl.BlockSpec((tm, WIDTH_A), lambda i: (i, 0)),
            pl.BlockSpec((tm, WIDTH_B), lambda i: (i, 0)),
            const(WIDTH_A + WIDTH_B, D),
            const(1, D),
            const(D, D_FF),
            const(D_FF, D),
        ],
        out_specs=pl.BlockSpec((tm, D), lambda i: (i, 0)),
        out_shape=jax.ShapeDtypeStruct((T, D), F32),
        compiler_params=pltpu.CompilerParams(
            dimension_semantics=("parallel",), vmem_limit_bytes=VMEM_LIMIT),
        name="out_proj_ffn",
    )(x, a, b, wo, g, w1, w2)


def _conv_ffn_kernel(x_ref, gn_ref, wp1_ref, wdw_ref, bdw_ref, gc_ref, wp2_ref,
                     g_ref, w1_ref, w2_ref, o_ref, ybuf):
    tm = x_ref.shape[1]
    x = x_ref[0]
    h = _rms_rows(x, gn_ref[...]).astype(BF16)
    z = jnp.dot(h, wp1_ref[...], preferred_element_type=F32)
    y = z[:, :D_MODEL] * _sigmoid(z[:, D_MODEL:])

    @pl.when(pl.program_id(1) == 0)
    def _():
        ybuf[0:CONV_HALO, :] = jnp.zeros((CONV_HALO, D_MODEL), F32)
        ybuf[CONV_HALO + tm:, :] = jnp.zeros((SUBLANES, D_MODEL), F32)

    ybuf[CONV_HALO:CONV_HALO + tm, :] = y
    first = CONV_HALO - (CONV_WIDTH - 1)
    span = tm + SUBLANES
    acc = None
    for res in reversed(range(SUBLANES)):
        part = None
        for j in range(CONV_WIDTH):
            if (first + j) % SUBLANES == res:
                base = first + j - res
                term = wdw_ref[j:j + 1, :] * ybuf[base:base + span, :]
                part = term if part is None else part + term
        acc = part if acc is None else part + pltpu.roll(acc, span - 1, axis=0)
    acc = acc[0:tm, :] + bdw_ref[...]
    ybuf[0:CONV_HALO, :] = ybuf[tm:tm + CONV_HALO, :]

    c = _rms_rows(acc, gc_ref[...])
    c = (c * _sigmoid(c)).astype(BF16)
    o_ref[0] = x + jnp.dot(c, wp2_ref[...], preferred_element_type=F32)
    o_ref[0] = _ffn(o_ref[0], g_ref[...], w1_ref, w2_ref)


def _conv_ffn(x, gn, wp1, wdw, bdw, gc, wp2, g, w1, w2, tm):
    B, S, D = x.shape
    const = lambda *shape: pl.BlockSpec(shape, lambda b, i: (0,) * len(shape),
                                        pipeline_mode=pl.Buffered(1))
    return pl.pallas_call(
        _conv_ffn_kernel,
        grid=(B, S // tm),
        in_specs=[
            pl.BlockSpec((1, tm, D), lambda b, i: (b, i, 0)),
            const(1, D),
            const(D, 2 * D),
            const(CONV_WIDTH, D),
            const(1, D),
            const(1, D),
            const(D, D),
            const(1, D),
            const(D, D_FF),
            const(D_FF, D),
        ],
        out_specs=pl.BlockSpec((1, tm, D), lambda b, i: (b, i, 0)),
        out_shape=jax.ShapeDtypeStruct((B, S, D), F32),
        scratch_shapes=[pltpu.VMEM((tm + CONV_HALO + SUBLANES, D), F32)],
        compiler_params=pltpu.CompilerParams(
            dimension_semantics=("arbitrary", "arbitrary"), vmem_limit_bytes=VMEM_LIMIT),
        name="conv_module_ffn",
    )(x, gn, wp1, wdw, bdw, gc, wp2, g, w1, w2)


def _column_plan():
    q0, k0, v0 = 0, WIDTH_A, WIDTH_A + HEAD_DIM
    qi0 = v0 + HEAD_DIM
    ki0 = qi0 + N_HEADS * IDX_DIM
    wi0 = ki0 + IDX_DIM
    u0 = wi0 + N_HEADS
    vb0 = u0 + WIDTH_B
    main = -np.ones((MAIN_COLS,), np.int64)
    swap = -np.ones((ROPE_COLS,), np.int64)

    def put(dst, src, width):
        half = width // 2
        for d in range(width):
            main[dst + d] = src + d
            swap[dst + d] = src + (d + half) % width

    for h in range(N_HEADS):
        put(h * LANES, q0 + h * HEAD_DIM, HEAD_DIM)
        put(h * LANES + HEAD_DIM, qi0 + h * IDX_DIM, IDX_DIM)
    put(N_HEADS * LANES, k0, HEAD_DIM)
    put(N_HEADS * LANES + HEAD_DIM, ki0, IDX_DIM)
    main[ROPE_COLS:ROPE_COLS + HEAD_DIM] = v0 + np.arange(HEAD_DIM)
    main[ROPE_COLS + HEAD_DIM:ROPE_COLS + HEAD_DIM + N_HEADS] = wi0 + np.arange(N_HEADS)
    main[ROPE_COLS + LANES:ROPE_COLS + LANES + WIDTH_B] = u0 + np.arange(WIDTH_B)
    main[ROPE_COLS + LANES + WIDTH_B:] = vb0 + np.arange(WIDTH_B)
    return main, swap


def _take_cols(w, idx):
    wz = jnp.concatenate([w, jnp.zeros((w.shape[0], 1), w.dtype)], axis=1)
    return jnp.take(wz, jnp.asarray(np.where(idx < 0, w.shape[1], idx)), axis=1)


def _rope_lane_tables(g_q, g_k, g_kidx):
    half_q, half_i = HEAD_DIM // 2, IDX_DIM // 2
    f_q = ROPE_THETA ** (-jnp.arange(half_q, dtype=F32) * 2.0 / HEAD_DIM)
    f_i = ROPE_THETA ** (-jnp.arange(half_i, dtype=F32) * 2.0 / IDX_DIM)
    pad = jnp.zeros((LANES - HEAD_DIM - IDX_DIM,), F32)
    freq = jnp.concatenate([f_q, f_q, f_i, f_i, pad])
    sgn = jnp.concatenate([-jnp.ones((half_q,), F32), jnp.ones((half_q,), F32),
                           -jnp.ones((half_i,), F32), jnp.ones((half_i,), F32), pad])
    swap_q = jnp.concatenate([g_q[half_q:], g_q[:half_q]])
    swap_k = jnp.concatenate([g_k[half_q:], g_k[:half_q]])
    swap_i = jnp.concatenate([g_kidx[half_i:], g_kidx[:half_i]])
    ones_i = jnp.ones((IDX_DIM,), F32)
    q_main = jnp.concatenate([g_q, ones_i, pad])
    q_swap = jnp.concatenate([swap_q, ones_i, pad])
    k_main = jnp.concatenate([g_k, g_kidx, pad])
    k_swap = jnp.concatenate([swap_k, swap_i, pad])
    gm = jnp.concatenate([jnp.tile(q_main, N_HEADS), k_main])
    gs = jnp.concatenate([jnp.tile(q_swap, N_HEADS), k_swap])
    return freq[None], sgn[None], gm[None], gs[None]


def kernel(x, positions, ev_g_norm, ev_w_in, ev_g_q, ev_g_k, ev_g_kidx, ev_g_vb, ev_w_s, ev_b_s,
           ev_w_out, od_g_norm, od_w_pw1, od_w_dw, od_b_dw, od_g_conv, od_w_pw2, ff_g_norm,
           ff_w1, ff_w2):
    B, S, D = x.shape
    assert D == D_MODEL and S % (2 * KEY_CHUNK) == 0
    tm = 512

    main_idx, swap_idx = _column_plan()
    w_in = ev_w_in[0]
    wm = _take_cols(w_in, main_idx).astype(BF16)
    ws = _take_cols(w_in, swap_idx).astype(BF16)
    freq, sgn, gm, gs = _rope_lane_tables(ev_g_q[0], ev_g_k[0], ev_g_kidx[0])
    vws = jnp.concatenate([jnp.ones((HEAD_DIM,), F32), jnp.full((N_HEADS,), IDX_SCALE, F32),
                           jnp.zeros((LANES - HEAD_DIM - N_HEADS,), F32)])[None]
    bsp = jnp.repeat(ev_b_s[0].T, GROUP_DIM, axis=1)
    pos = positions.astype(F32)[..., None]

    r, kk, vt, wt, b_out = _in_stage(
        x, pos, ev_g_norm[0][None], wm, ws, freq, sgn, gm, gs, vws, ev_g_vb[0][None],
        ev_w_s[0], bsp, tm)
    a_out = _dsa(r, kk.reshape(B, S // KEY_CHUNK, KEY_CHUNK, LANES), vt, wt)
    x1 = _out_ffn(x.reshape(B * S, D), a_out.reshape(B * S, WIDTH_A), b_out.reshape(B * S, WIDTH_B),
                  ev_w_out[0].astype(BF16), ff_g_norm[0][None], ff_w1[0].astype(BF16),
                  ff_w2[0].astype(BF16), tm)

    x2 = _conv_ffn(x1.reshape(B, S, D), od_g_norm[0][None], od_w_pw1[0].astype(BF16), od_w_dw[0],
                   od_b_dw[0][None], od_g_conv[0][None], od_w_pw2[0].astype(BF16),
                   ff_g_norm[1][None], ff_w1[1].astype(BF16), ff_w2[1].astype(BF16), tm)
    return x2
```

```python
import functools

import numpy as np
import jax
import jax.numpy as jnp
from jax import lax
from jax.experimental import pallas as pl
from jax.experimental.pallas import tpu as pltpu

F32 = jnp.float32
BF16 = jnp.bfloat16

D_MODEL = 1024
N_HEADS = 8
HEAD_DIM = 64
IDX_DIM = 32
TOPK = 256
Q_BLOCK = 128
N_GROUPS = 8
GROUP_DIM = 64
WIDTH_A = N_HEADS * HEAD_DIM
WIDTH_B = N_GROUPS * GROUP_DIM
CHUNK = 128
CONV_WIDTH = 31
D_FF = 4 * D_MODEL
ROPE_THETA = 10000.0
EPS = 1e-6
IDX_SCALE = (N_HEADS ** -0.5) * (IDX_DIM ** -0.5)

LANES = 128
SUBLANES = 8
KEY_CHUNK = 512
LOG2_E = 1.4426950408889634
COUNT_ROWS = 64
V_ROWS = HEAD_DIM + 16
ROPE_COLS = (N_HEADS + 1) * LANES
MAIN_COLS = ROPE_COLS + LANES + 2 * WIDTH_B
CONV_HALO = 32
VMEM_LIMIT = 56 * 1024 * 1024


def _rms_rows(x, g):
    ms = jnp.mean(x * x, axis=-1, keepdims=True)
    return x * lax.rsqrt(ms + EPS) * g


def _sigmoid(x):
    return 1.0 / (1.0 + jnp.exp(-x))


def _gelu_tanh(x):
    c = np.float32(np.sqrt(2.0 / np.pi))
    return 0.5 * x * (1.0 + jnp.tanh(c * (x + 0.044715 * (x * x * x))))


def _ffn(x, g, w1_ref, w2_ref):
    h = _rms_rows(x, g).astype(BF16)
    acc = x
    slab = D_FF // 4
    for c in range(4):
        t = jnp.dot(h, w1_ref[:, c * slab:(c + 1) * slab], preferred_element_type=F32)
        t = jnp.maximum(t, 0.0)
        t = (t * t).astype(BF16)
        acc = acc + jnp.dot(t, w2_ref[c * slab:(c + 1) * slab, :], preferred_element_type=F32)
    return acc


def _in_stage_kernel(x_ref, pos_ref, gn_ref, wm_ref, ws_ref, freq_ref, sgn_ref,
                     gm_ref, gs_ref, vws_ref, gvb_ref, wsp_ref, bsp_ref,
                     r_ref, kk_ref, vt_ref, wt_ref, b_ref):
    tm = x_ref.shape[1]
    x = x_ref[0]
    h = _rms_rows(x, gn_ref[...]).astype(BF16)
    z = jnp.dot(h, wm_ref[...], preferred_element_type=F32)
    zs = jnp.dot(h, ws_ref[...], preferred_element_type=F32)

    lane = lax.broadcasted_iota(jnp.int32, (tm, LANES), 1)
    ang = pos_ref[0] * freq_ref[...]
    cos = jnp.cos(ang)
    sin = jnp.sin(ang) * sgn_ref[...]
    is_main = lane < HEAD_DIM

    for blk in range(N_HEADS + 1):
        sl = slice(blk * LANES, (blk + 1) * LANES)
        zb = z[:, sl]
        zw = zs[:, sl]
        sq = zb * zb
        ms_a = jnp.sum(jnp.where(is_main, sq, 0.0), axis=-1, keepdims=True) * (1.0 / HEAD_DIM)
        if blk < N_HEADS:
            inv = jnp.where(is_main, lax.rsqrt(ms_a + EPS), 1.0)
        else:
            ms_b = jnp.sum(jnp.where(is_main, 0.0, sq), axis=-1, keepdims=True) * (1.0 / IDX_DIM)
            inv = jnp.where(is_main, lax.rsqrt(ms_a + EPS), lax.rsqrt(ms_b + EPS))
        y = (zb * gm_ref[:, sl]) * inv
        yw = (zw * gs_ref[:, sl]) * inv
        out = (y * cos + yw * sin).astype(BF16)
        if blk < N_HEADS:
            r_ref[0, :, sl] = out
        else:
            kk_ref[0] = out

    vw = z[:, ROPE_COLS:ROPE_COLS + LANES] * vws_ref[...]
    vwt = vw.T
    for j in range(tm // KEY_CHUNK):
        vt_ref[0, j] = jnp.concatenate(
            [vwt[0:HEAD_DIM, j * KEY_CHUNK:(j + 1) * KEY_CHUNK],
             jnp.ones((V_ROWS - HEAD_DIM, KEY_CHUNK), F32)], axis=0).astype(BF16)
    wt_ref[0] = vwt[HEAD_DIM:HEAD_DIM + N_HEADS, :]

    u0 = ROPE_COLS + LANES
    ug = _gelu_tanh(z[:, u0:u0 + WIDTH_B])
    vg = _gelu_tanh(z[:, u0 + WIDTH_B:u0 + 2 * WIDTH_B])
    vn = _rms_rows(vg, gvb_ref[...]).astype(BF16)
    row = lax.broadcasted_iota(jnp.int32, (CHUNK, CHUNK), 0)
    col = lax.broadcasted_iota(jnp.int32, (CHUNK, CHUNK), 1)
    tril = col <= row
    wmix = [jnp.where(tril, wsp_ref[g], 0.0).astype(BF16) for g in range(N_GROUPS)]
    low = lax.broadcasted_iota(jnp.int32, (CHUNK, LANES), 1) < GROUP_DIM
    for c in range(tm // CHUNK):
        rows = slice(c * CHUNK, (c + 1) * CHUNK)
        for p in range(N_GROUPS // 2):
            cols = slice(p * LANES, (p + 1) * LANES)
            vb = vn[rows, cols]
            s0 = jnp.dot(wmix[2 * p], vb, preferred_element_type=F32)
            s1 = jnp.dot(wmix[2 * p + 1], vb, preferred_element_type=F32)
            sv = jnp.where(low, s0, s1) + bsp_ref[:, cols]
            b_ref[0, rows, cols] = (ug[rows, cols] * sv).astype(BF16)


def _in_stage(x, pos, gn, wm, ws, freq, sgn, gm, gs, vws, gvb, wsp, bsp, tm):
    B, S, D = x.shape
    n_kc = S // KEY_CHUNK
    const = lambda *shape: pl.BlockSpec(shape, lambda b, i: (0,) * len(shape),
                                        pipeline_mode=pl.Buffered(1))
    return pl.pallas_call(
        _in_stage_kernel,
        grid=(B, S // tm),
        in_specs=[
            pl.BlockSpec((1, tm, D), lambda b, i: (b, i, 0)),
            pl.BlockSpec((1, tm, 1), lambda b, i: (b, i, 0)),
            const(1, D),
            const(D, MAIN_COLS),
            const(D, ROPE_COLS),
            const(1, LANES),
            const(1, LANES),
            const(1, ROPE_COLS),
            const(1, ROPE_COLS),
            const(1, LANES),
            const(1, WIDTH_B),
            const(N_GROUPS, CHUNK, CHUNK),
            const(CHUNK, WIDTH_B),
        ],
        out_specs=[
            pl.BlockSpec((1, tm, N_HEADS * LANES), lambda b, i: (b, i, 0)),
            pl.BlockSpec((1, tm, LANES), lambda b, i: (b, i, 0)),
            pl.BlockSpec((1, tm // KEY_CHUNK, V_ROWS, KEY_CHUNK), lambda b, i: (b, i, 0, 0)),
            pl.BlockSpec((1, N_HEADS, tm), lambda b, i: (b, 0, i)),
            pl.BlockSpec((1, tm, WIDTH_B), lambda b, i: (b, i, 0)),
        ],
        out_shape=[
            jax.ShapeDtypeStruct((B, S, N_HEADS * LANES), BF16),
            jax.ShapeDtypeStruct((B, S, LANES), BF16),
            jax.ShapeDtypeStruct((B, n_kc, V_ROWS, KEY_CHUNK), BF16),
            jax.ShapeDtypeStruct((B, N_HEADS, S), F32),
            jax.ShapeDtypeStruct((B, S, WIDTH_B), BF16),
        ],
        compiler_params=pltpu.CompilerParams(
            dimension_semantics=("parallel", "parallel"), vmem_limit_bytes=VMEM_LIMIT),
        name="even_in_stage",
    )(x, pos, gn, wm, ws, freq, sgn, gm, gs, vws, gvb, wsp, bsp)


def _key_to_float(u):
    key = u ^ jnp.int32(-2 ** 31)
    bits = jnp.where(key >= 0, key, key ^ jnp.int32(0x7FFFFFFF))
    return lax.bitcast_convert_type(bits, jnp.float32)


def _dsa_kernel(r_ref, kk_ref, vt_ref, wt_ref, o_ref, sc_ref, buf_a, buf_b):
    qb = pl.program_id(1)
    n_ch = (qb * Q_BLOCK + Q_BLOCK + KEY_CHUNK - 1) // KEY_CHUNK
    n_pair = (n_ch + 1) // 2

    r = r_ref[0].astype(F32)
    lane = lax.broadcasted_iota(jnp.int32, (Q_BLOCK, LANES), 1)
    is_q = lane < HEAD_DIM
    q_scale = (HEAD_DIM ** -0.5) * LOG2_E
    rq = jnp.concatenate(
        [jnp.where(is_q, r[:, h * LANES:(h + 1) * LANES] * q_scale, 0.0)
         for h in range(N_HEADS)], axis=0).astype(BF16)
    rqi = jnp.concatenate(
        [jnp.where(is_q, 0.0, r[:, h * LANES:(h + 1) * LANES]) for h in range(N_HEADS)],
        axis=0).astype(BF16)
    wt = wt_ref[0]
    t_idx = qb * Q_BLOCK + lax.broadcasted_iota(jnp.int32, (1, LANES), 1)
    row_iota = lax.broadcasted_iota(jnp.int32, (KEY_CHUNK, LANES), 0)
    nt = (((1,), (1,)), ((), ()))

    last = kk_ref.shape[1] - 1

    def keys_dot(c, w):
        return lax.dot_general(kk_ref[0, jnp.minimum(c, last)], w, nt, preferred_element_type=F32)


    def write_scores(c, lg_ref):
        sc = wt[0:1, :] * jnp.maximum(lg_ref[:, 0:LANES], 0.0)
        for h in range(1, N_HEADS):
            sc = sc + wt[h:h + 1, :] * jnp.maximum(lg_ref[:, h * LANES:(h + 1) * LANES], 0.0)
        s_idx = c * KEY_CHUNK + row_iota
        sc_ref[c] = jnp.where(s_idx <= t_idx, sc, -jnp.inf)

    def score_pair(j, carry):
        buf_b[...] = keys_dot(2 * j + 1, rqi)
        write_scores(2 * j, buf_a)
        buf_a[...] = keys_dot(2 * j + 2, rqi)
        write_scores(2 * j + 1, buf_b)
        return carry

    n_full = n_ch // 2
    odd = n_ch - 2 * n_full
    buf_a[...] = keys_dot(0, rqi)
    lax.fori_loop(0, n_full, score_pair, 0)

    @pl.when(odd == 1)
    def _():
        write_scores(n_ch - 1, buf_a)
        sc_ref[n_ch] = jnp.full((KEY_CHUNK, LANES), -jnp.inf, F32)

    k_row = jnp.minimum(t_idx + 1, TOPK).astype(F32)

    slabs = [slice(s * COUNT_ROWS, (s + 1) * COUNT_ROWS) for s in range(KEY_CHUNK // COUNT_ROWS)]

    def count_rows(pred):
        def body(j, acc):
            for half in range(2):
                for rows in slabs:
                    acc = acc + jnp.where(pred(sc_ref[2 * j + half, rows, :]), 1.0, 0.0)
            return acc
        acc = lax.fori_loop(0, n_pair, body, jnp.zeros((COUNT_ROWS, LANES), F32))
        return acc.sum(axis=0, keepdims=True)

    def bit_body(i, carry):
        u, cnt_u = carry
        trial = u | lax.shift_left(jnp.int32(1), jnp.asarray(31 - i, jnp.int32))
        cvec = _key_to_float(trial)
        cnt = count_rows(lambda x: x >= cvec)
        take = cnt >= k_row
        return jnp.where(take, trial, u), jnp.where(take, cnt, cnt_u)

    u, cnt_thr = lax.fori_loop(
        0, 32, bit_body, (jnp.zeros((1, LANES), jnp.int32), jnp.zeros((1, LANES), F32)))
    thr = _key_to_float(u)

    @pl.when(jnp.max(cnt_thr - k_row) > 0.0)
    def _():
        n_tie = count_rows(lambda x: x == thr)
        need = k_row - (cnt_thr - n_tie)
        ri = lax.broadcasted_iota(jnp.int32, (KEY_CHUNK, KEY_CHUNK), 0)
        ci = lax.broadcasted_iota(jnp.int32, (KEY_CHUNK, KEY_CHUNK), 1)
        before = jnp.where(ci < ri, 1.0, 0.0).astype(BF16)

        def drop_body(c, seen):
            x = sc_ref[c]
            tie = x == thr
            tie_f = jnp.where(tie, 1.0, 0.0)
            rank = seen + jnp.dot(before, tie_f.astype(BF16), preferred_element_type=F32)
            sc_ref[c] = jnp.where(tie, jnp.where(rank >= need, -jnp.inf, x), x)
            return seen + jnp.sum(tie_f, axis=0, keepdims=True)

        lax.fori_loop(0, n_ch, drop_body, jnp.zeros((1, LANES), F32))

    def attend(c, st_ref, carry):
        ms, accs = carry
        vt = vt_ref[0, c]
        sel = sc_ref[c] >= thr
        new_ms, new_accs = [], []
        for pr in range(N_HEADS // 2):
            ps, alphas = [], []
            for hh in range(2):
                h = 2 * pr + hh
                m_old = ms[h]
                s = jnp.where(sel, st_ref[:, h * LANES:(h + 1) * LANES], -jnp.inf)
                m_new = jnp.maximum(m_old, jnp.max(s, axis=0, keepdims=True))
                ps.append(jnp.exp2(s - m_new).astype(BF16))
                alphas.append(jnp.exp2(m_old - m_new))
                new_ms.append(m_new)
            pv = jnp.dot(vt, jnp.concatenate(ps, axis=1), preferred_element_type=F32)
            new_accs.append(jnp.concatenate(alphas, axis=1) * accs[pr] + pv)
        return tuple(new_ms), tuple(new_accs)

    def attn_pair(j, carry):
        buf_b[...] = keys_dot(2 * j + 1, rq)
        carry = attend(2 * j, buf_a, carry)
        buf_a[...] = keys_dot(2 * j + 2, rq)
        return attend(2 * j + 1, buf_b, carry)

    ms0 = tuple(jnp.full((1, LANES), -1e30, F32) for _ in range(N_HEADS))
    accs0 = tuple(jnp.zeros((V_ROWS, 2 * LANES), F32) for _ in range(N_HEADS // 2))
    buf_a[...] = keys_dot(0, rq)
    carry = lax.fori_loop(0, n_full, attn_pair, (ms0, accs0))
    _, accs = lax.cond(odd == 1, lambda cr: attend(n_ch - 1, buf_a, cr), lambda cr: cr, carry)
    outs = []
    for h in range(N_HEADS):
        a = accs[h // 2][:, (h % 2) * LANES:(h % 2 + 1) * LANES]
        outs.append(a[0:HEAD_DIM] * (1.0 / a[HEAD_DIM:HEAD_DIM + 1]))
    o_ref[0] = jnp.concatenate(outs, axis=0).T.astype(BF16)


def _dsa(r, kk, vt, wt):
    B, S, _ = r.shape
    n_kc = S // KEY_CHUNK
    return pl.pallas_call(
        _dsa_kernel,
        grid=(B, S // Q_BLOCK),
        in_specs=[
            pl.BlockSpec((1, Q_BLOCK, N_HEADS * LANES), lambda b, i: (b, i, 0)),
            pl.BlockSpec((1, n_kc, KEY_CHUNK, LANES), lambda b, i: (b, 0, 0, 0)),
            pl.BlockSpec((1, n_kc, V_ROWS, KEY_CHUNK), lambda b, i: (b, 0, 0, 0)),
            pl.BlockSpec((1, N_HEADS, Q_BLOCK), lambda b, i: (b, 0, i)),
        ],
        out_specs=pl.BlockSpec((1, Q_BLOCK, WIDTH_A), lambda b, i: (b, i, 0)),
        out_shape=jax.ShapeDtypeStruct((B, S, WIDTH_A), BF16),
        scratch_shapes=[pltpu.VMEM((n_kc + 1, KEY_CHUNK, LANES), F32),
                        pltpu.VMEM((KEY_CHUNK, N_HEADS * LANES), F32),
                        pltpu.VMEM((KEY_CHUNK, N_HEADS * LANES), F32)],
        compiler_params=pltpu.CompilerParams(
            dimension_semantics=("parallel", "parallel"), vmem_limit_bytes=VMEM_LIMIT),
        name="dsa_attention",
    )(r, kk, vt, wt)


def _out_ffn_kernel(x_ref, a_ref, b_ref, wo_ref, g_ref, w1_ref, w2_ref, o_ref):
    ab = jnp.concatenate([a_ref[...], b_ref[...]], axis=1)
    o_ref[...] = x_ref[...] + jnp.dot(ab, wo_ref[...], preferred_element_type=F32)
    o_ref[...] = _ffn(o_ref[...], g_ref[...], w1_ref, w2_ref)


def _out_ffn(x, a, b, wo, g, w1, w2, tm):
    T, D = x.shape
    const = lambda *shape: pl.BlockSpec(shape, lambda i: (0,) * len(shape),
                                        pipeline_mode=pl.Buffered(1))
    return pl.pallas_call(
        _out_ffn_kernel,
        grid=(T // tm,),
        in_specs=[
            pl.BlockSpec((tm, D), lambda i: (i, 0)),
            pl.BlockSpec((tm, WIDTH_A), lambda i: (i, 0)),
            pl.BlockSpec((tm, WIDTH_B), lambda i: (i, 0)),
            const(WIDTH_A + WIDTH_B, D),
            const(1, D),
            const(D, D_FF),
            const(D_FF, D),
        ],
        out_specs=pl.BlockSpec((tm, D), lambda i: (i, 0)),
        out_shape=jax.ShapeDtypeStruct((T, D), F32),
        compiler_params=pltpu.CompilerParams(
            dimension_semantics=("parallel",), vmem_limit_bytes=VMEM_LIMIT),
        name="out_proj_ffn",
    )(x, a, b, wo, g, w1, w2)


def _conv_ffn_kernel(x_ref, gn_ref, wp1_ref, wdw_ref, bdw_ref, gc_ref, wp2_ref,
                     g_ref, w1_ref, w2_ref, o_ref, ybuf):
    tm = x_ref.shape[1]
    x = x_ref[0]
    h = _rms_rows(x, gn_ref[...]).astype(BF16)
    z = jnp.dot(h, wp1_ref[...], preferred_element_type=F32)
    y = z[:, :D_MODEL] * _sigmoid(z[:, D_MODEL:])

    @pl.when(pl.program_id(1) == 0)
    def _():
        ybuf[0:CONV_HALO, :] = jnp.zeros((CONV_HALO, D_MODEL), F32)
        ybuf[CONV_HALO + tm:, :] = jnp.zeros((SUBLANES, D_MODEL), F32)

    ybuf[CONV_HALO:CONV_HALO + tm, :] = y
    first = CONV_HALO - (CONV_WIDTH - 1)
    span = tm + SUBLANES
    acc = None
    for res in reversed(range(SUBLANES)):
        part = None
        for j in range(CONV_WIDTH):
            if (first + j) % SUBLANES == res:
                base = first + j - res
                term = wdw_ref[j:j + 1, :] * ybuf[base:base + span, :]
                part = term if part is None else part + term
        acc = part if acc is None else part + pltpu.roll(acc, span - 1, axis=0)
    acc = acc[0:tm, :] + bdw_ref[...]
    ybuf[0:CONV_HALO, :] = ybuf[tm:tm + CONV_HALO, :]

    c = _rms_rows(acc, gc_ref[...])
    c = (c * _sigmoid(c)).astype(BF16)
    o_ref[0] = x + jnp.dot(c, wp2_ref[...], preferred_element_type=F32)
    o_ref[0] = _ffn(o_ref[0], g_ref[...], w1_ref, w2_ref)


def _conv_ffn(x, gn, wp1, wdw, bdw, gc, wp2, g, w1, w2, tm):
    B, S, D = x.shape
    const = lambda *shape: pl.BlockSpec(shape, lambda b, i: (0,) * len(shape),
                                        pipeline_mode=pl.Buffered(1))
    return pl.pallas_call(
        _conv_ffn_kernel,
        grid=(B, S // tm),
        in_specs=[
            pl.BlockSpec((1, tm, D), lambda b, i: (b, i, 0)),
            const(1, D),
            const(D, 2 * D),
            const(CONV_WIDTH, D),
            const(1, D),
            const(1, D),
            const(D, D),
            const(1, D),
            const(D, D_FF),
            const(D_FF, D),
        ],
        out_specs=pl.BlockSpec((1, tm, D), lambda b, i: (b, i, 0)),
        out_shape=jax.ShapeDtypeStruct((B, S, D), F32),
        scratch_shapes=[pltpu.VMEM((tm + CONV_HALO + SUBLANES, D), F32)],
        compiler_params=pltpu.CompilerParams(
            dimension_semantics=("arbitrary", "arbitrary"), vmem_limit_bytes=VMEM_LIMIT),
        name="conv_module_ffn",
    )(x, gn, wp1, wdw, bdw, gc, wp2, g, w1, w2)


def _column_plan():
    q0, k0, v0 = 0, WIDTH_A, WIDTH_A + HEAD_DIM
    qi0 = v0 + HEAD_DIM
    ki0 = qi0 + N_HEADS * IDX_DIM
    wi0 = ki0 + IDX_DIM
    u0 = wi0 + N_HEADS
    vb0 = u0 + WIDTH_B
    main = -np.ones((MAIN_COLS,), np.int64)
    swap = -np.ones((ROPE_COLS,), np.int64)

    def put(dst, src, width):
        half = width // 2
        for d in range(width):
            main[dst + d] = src + d
            swap[dst + d] = src + (d + half) % width

    for h in range(N_HEADS):
        put(h * LANES, q0 + h * HEAD_DIM, HEAD_DIM)
        put(h * LANES + HEAD_DIM, qi0 + h * IDX_DIM, IDX_DIM)
    put(N_HEADS * LANES, k0, HEAD_DIM)
    put(N_HEADS * LANES + HEAD_DIM, ki0, IDX_DIM)
    main[ROPE_COLS:ROPE_COLS + HEAD_DIM] = v0 + np.arange(HEAD_DIM)
    main[ROPE_COLS + HEAD_DIM:ROPE_COLS + HEAD_DIM + N_HEADS] = wi0 + np.arange(N_HEADS)
    main[ROPE_COLS + LANES:ROPE_COLS + LANES + WIDTH_B] = u0 + np.arange(WIDTH_B)
    main[ROPE_COLS + LANES + WIDTH_B:] = vb0 + np.arange(WIDTH_B)
    return main, swap


def _take_cols(w, idx):
    wz = jnp.concatenate([w, jnp.zeros((w.shape[0], 1), w.dtype)], axis=1)
    return jnp.take(wz, jnp.asarray(np.where(idx < 0, w.shape[1], idx)), axis=1)


def _rope_lane_tables(g_q, g_k, g_kidx):
    half_q, half_i = HEAD_DIM // 2, IDX_DIM // 2
    f_q = ROPE_THETA ** (-jnp.arange(half_q, dtype=F32) * 2.0 / HEAD_DIM)
    f_i = ROPE_THETA ** (-jnp.arange(half_i, dtype=F32) * 2.0 / IDX_DIM)
    pad = jnp.zeros((LANES - HEAD_DIM - IDX_DIM,), F32)
    freq = jnp.concatenate([f_q, f_q, f_i, f_i, pad])
    sgn = jnp.concatenate([-jnp.ones((half_q,), F32), jnp.ones((half_q,), F32),
                           -jnp.ones((half_i,), F32), jnp.ones((half_i,), F32), pad])
    swap_q = jnp.concatenate([g_q[half_q:], g_q[:half_q]])
    swap_k = jnp.concatenate([g_k[half_q:], g_k[:half_q]])
    swap_i = jnp.concatenate([g_kidx[half_i:], g_kidx[:half_i]])
    ones_i = jnp.ones((IDX_DIM,), F32)
    q_main = jnp.concatenate([g_q, ones_i, pad])
    q_swap = jnp.concatenate([swap_q, ones_i, pad])
    k_main = jnp.concatenate([g_k, g_kidx, pad])
    k_swap = jnp.concatenate([swap_k, swap_i, pad])
    gm = jnp.concatenate([jnp.tile(q_main, N_HEADS), k_main])
    gs = jnp.concatenate([jnp.tile(q_swap, N_HEADS), k_swap])
    return freq[None], sgn[None], gm[None], gs[None]


def kernel(x, positions, ev_g_norm, ev_w_in, ev_g_q, ev_g_k, ev_g_kidx, ev_g_vb, ev_w_s, ev_b_s,
           ev_w_out, od_g_norm, od_w_pw1, od_w_dw, od_b_dw, od_g_conv, od_w_pw2, ff_g_norm,
           ff_w1, ff_w2):
    B, S, D = x.shape
    assert D == D_MODEL and S % (2 * KEY_CHUNK) == 0
    tm = 512

    main_idx, swap_idx = _column_plan()
    w_in = ev_w_in[0]
    wm = _take_cols(w_in, main_idx).astype(BF16)
    ws = _take_cols(w_in, swap_idx).astype(BF16)
    freq, sgn, gm, gs = _rope_lane_tables(ev_g_q[0], ev_g_k[0], ev_g_kidx[0])
    vws = jnp.concatenate([jnp.ones((HEAD_DIM,), F32), jnp.full((N_HEADS,), IDX_SCALE, F32),
                           jnp.zeros((LANES - HEAD_DIM - N_HEADS,), F32)])[None]
    bsp = jnp.repeat(ev_b_s[0].T, GROUP_DIM, axis=1)
    pos = positions.astype(F32)[..., None]

    r, kk, vt, wt, b_out = _in_stage(
        x, pos, ev_g_norm[0][None], wm, ws, freq, sgn, gm, gs, vws, ev_g_vb[0][None],
        ev_w_s[0], bsp, tm)
    a_out = _dsa(r, kk.reshape(B, S // KEY_CHUNK, KEY_CHUNK, LANES), vt, wt)
    x1 = _out_ffn(x.reshape(B * S, D), a_out.reshape(B * S, WIDTH_A), b_out.reshape(B * S, WIDTH_B),
                  ev_w_out[0].astype(BF16), ff_g_norm[0][None], ff_w1[0].astype(BF16),
                  ff_w2[0].astype(BF16), tm)

    x2 = _conv_ffn(x1.reshape(B, S, D), od_g_norm[0][None], od_w_pw1[0].astype(BF16), od_w_dw[0],
                   od_b_dw[0][None], od_g_conv[0][None], od_w_pw2[0].astype(BF16),
                   ff_g_norm[1][None], ff_w1[1].astype(BF16), ff_w2[1].astype(BF16), tm)
    return x2
```

```python
import functools

import numpy as np
import jax
import jax.numpy as jnp
from jax import lax
from jax.experimental import pallas as pl
from jax.experimental.pallas import tpu as pltpu

F32 = jnp.float32
BF16 = jnp.bfloat16

D_MODEL = 1024
N_HEADS = 8
HEAD_DIM = 64
IDX_DIM = 32
TOPK = 256
Q_BLOCK = 128
N_GROUPS = 8
GROUP_DIM = 64
WIDTH_A = N_HEADS * HEAD_DIM
WIDTH_B = N_GROUPS * GROUP_DIM
CHUNK = 128
CONV_WIDTH = 31
D_FF = 4 * D_MODEL
ROPE_THETA = 10000.0
EPS = 1e-6
IDX_SCALE = (N_HEADS ** -0.5) * (IDX_DIM ** -0.5)

LANES = 128
SUBLANES = 8
KEY_CHUNK = 512
LOG2_E = 1.4426950408889634
COUNT_ROWS = 64
V_ROWS = HEAD_DIM + 16
ROPE_COLS = (N_HEADS + 1) * LANES
MAIN_COLS = ROPE_COLS + LANES + 2 * WIDTH_B
CONV_HALO = 32
VMEM_LIMIT = 56 * 1024 * 1024


def _rms_rows(x, g):
    ms = jnp.mean(x * x, axis=-1, keepdims=True)
    return x * lax.rsqrt(ms + EPS) * g


def _sigmoid(x):
    return 1.0 / (1.0 + jnp.exp(-x))


def _gelu_tanh(x):
    c = np.float32(np.sqrt(2.0 / np.pi))
    return 0.5 * x * (1.0 + jnp.tanh(c * (x + 0.044715 * (x * x * x))))


def _ffn(x, g, w1_ref, w2_ref):
    h = _rms_rows(x, g).astype(BF16)
    acc = x
    slab = D_FF // 4
    for c in range(4):
        t = jnp.dot(h, w1_ref[:, c * slab:(c + 1) * slab], preferred_element_type=F32)
        t = jnp.maximum(t, 0.0)
        t = (t * t).astype(BF16)
        acc = acc + jnp.dot(t, w2_ref[c * slab:(c + 1) * slab, :], preferred_element_type=F32)
    return acc


def _in_stage_kernel(x_ref, pos_ref, gn_ref, wm_ref, ws_ref, freq_ref, sgn_ref,
                     gm_ref, gs_ref, vws_ref, gvb_ref, wsp_ref, bsp_ref,
                     r_ref, kk_ref, vt_ref, wt_ref, b_ref):
    tm = x_ref.shape[1]
    x = x_ref[0]
    h = _rms_rows(x, gn_ref[...]).astype(BF16)
    z = jnp.dot(h, wm_ref[...], preferred_element_type=F32)
    zs = jnp.dot(h, ws_ref[...], preferred_element_type=F32)

    lane = lax.broadcasted_iota(jnp.int32, (tm, LANES), 1)
    ang = pos_ref[0] * freq_ref[...]
    cos = jnp.cos(ang)
    sin = jnp.sin(ang) * sgn_ref[...]
    is_main = lane < HEAD_DIM

    for blk in range(N_HEADS + 1):
        sl = slice(blk * LANES, (blk + 1) * LANES)
        zb = z[:, sl]
        zw = zs[:, sl]
        sq = zb * zb
        ms_a = jnp.sum(jnp.where(is_main, sq, 0.0), axis=-1, keepdims=True) * (1.0 / HEAD_DIM)
        if blk < N_HEADS:
            inv = jnp.where(is_main, lax.rsqrt(ms_a + EPS), 1.0)
        else:
            ms_b = jnp.sum(jnp.where(is_main, 0.0, sq), axis=-1, keepdims=True) * (1.0 / IDX_DIM)
            inv = jnp.where(is_main, lax.rsqrt(ms_a + EPS), lax.rsqrt(ms_b + EPS))
        y = (zb * gm_ref[:, sl]) * inv
        yw = (zw * gs_ref[:, sl]) * inv
        out = (y * cos + yw * sin).astype(BF16)
        if blk < N_HEADS:
            r_ref[0, :, sl] = out
        else:
            kk_ref[0] = out

    vw = z[:, ROPE_COLS:ROPE_COLS + LANES] * vws_ref[...]
    vwt = vw.T
    for j in range(tm // KEY_CHUNK):
        vt_ref[0, j] = jnp.concatenate(
            [vwt[0:HEAD_DIM, j * KEY_CHUNK:(j + 1) * KEY_CHUNK],
             jnp.ones((V_ROWS - HEAD_DIM, KEY_CHUNK), F32)], axis=0).astype(BF16)
    wt_ref[0] = vwt[HEAD_DIM:HEAD_DIM + N_HEADS, :]

    u0 = ROPE_COLS + LANES
    ug = _gelu_tanh(z[:, u0:u0 + WIDTH_B])
    vg = _gelu_tanh(z[:, u0 + WIDTH_B:u0 + 2 * WIDTH_B])
    vn = _rms_rows(vg, gvb_ref[...]).astype(BF16)
    row = lax.broadcasted_iota(jnp.int32, (CHUNK, CHUNK), 0)
    col = lax.broadcasted_iota(jnp.int32, (CHUNK, CHUNK), 1)
    tril = col <= row
    wmix = [jnp.where(tril, wsp_ref[g], 0.0).astype(BF16) for g in range(N_GROUPS)]
    low = lax.broadcasted_iota(jnp.int32, (CHUNK, LANES), 1) < GROUP_DIM
    for c in range(tm // CHUNK):
        rows = slice(c * CHUNK, (c + 1) * CHUNK)
        for p in range(N_GROUPS // 2):
            cols = slice(p * LANES, (p + 1) * LANES)
            vb = vn[rows, cols]
            s0 = jnp.dot(wmix[2 * p], vb, preferred_element_type=F32)
            s1 = jnp.dot(wmix[2 * p + 1], vb, preferred_element_type=F32)
            sv = jnp.where(low, s0, s1) + bsp_ref[:, cols]
            b_ref[0, rows, cols] = (ug[rows, cols] * sv).astype(BF16)


def _in_stage(x, pos, gn, wm, ws, freq, sgn, gm, gs, vws, gvb, wsp, bsp, tm):
    B, S, D = x.shape
    n_kc = S // KEY_CHUNK
    const = lambda *shape: pl.BlockSpec(shape, lambda b, i: (0,) * len(shape),
                                        pipeline_mode=pl.Buffered(1))
    return pl.pallas_call(
        _in_stage_kernel,
        grid=(B, S // tm),
        in_specs=[
            pl.BlockSpec((1, tm, D), lambda b, i: (b, i, 0)),
            pl.BlockSpec((1, tm, 1), lambda b, i: (b, i, 0)),
            const(1, D),
            const(D, MAIN_COLS),
            const(D, ROPE_COLS),
            const(1, LANES),
            const(1, LANES),
            const(1, ROPE_COLS),
            const(1, ROPE_COLS),
            const(1, LANES),
            const(1, WIDTH_B),
            const(N_GROUPS, CHUNK, CHUNK),
            const(CHUNK, WIDTH_B),
        ],
        out_specs=[
            pl.BlockSpec((1, tm, N_HEADS * LANES), lambda b, i: (b, i, 0)),
            pl.BlockSpec((1, tm, LANES), lambda b, i: (b, i, 0)),
            pl.BlockSpec((1, tm // KEY_CHUNK, V_ROWS, KEY_CHUNK), lambda b, i: (b, i, 0, 0)),
            pl.BlockSpec((1, N_HEADS, tm), lambda b, i: (b, 0, i)),
            pl.BlockSpec((1, tm, WIDTH_B), lambda b, i: (b, i, 0)),
        ],
        out_shape=[
            jax.ShapeDtypeStruct((B, S, N_HEADS * LANES), BF16),
            jax.ShapeDtypeStruct((B, S, LANES), BF16),
            jax.ShapeDtypeStruct((B, n_kc, V_ROWS, KEY_CHUNK), BF16),
            jax.ShapeDtypeStruct((B, N_HEADS, S), F32),
            jax.ShapeDtypeStruct((B, S, WIDTH_B), BF16),
        ],
        compiler_params=pltpu.CompilerParams(
            dimension_semantics=("parallel", "parallel"), vmem_limit_bytes=VMEM_LIMIT),
        name="even_in_stage",
    )(x, pos, gn, wm, ws, freq, sgn, gm, gs, vws, gvb, wsp, bsp)


def _key_to_float(u):
    key = u ^ jnp.int32(-2 ** 31)
    bits = jnp.where(key >= 0, key, key ^ jnp.int32(0x7FFFFFFF))
    return lax.bitcast_convert_type(bits, jnp.float32)


def _dsa_kernel(r_ref, kk_ref, vt_ref, wt_ref, o_ref, sc_ref, buf_a, buf_b):
    qb = pl.program_id(1)
    n_ch = (qb * Q_BLOCK + Q_BLOCK + KEY_CHUNK - 1) // KEY_CHUNK
    n_pair = (n_ch + 1) // 2

    r = r_ref[0].astype(F32)
    lane = lax.broadcasted_iota(jnp.int32, (Q_BLOCK, LANES), 1)
    is_q = lane < HEAD_DIM
    q_scale = (HEAD_DIM ** -0.5) * LOG2_E
    rq = jnp.concatenate(
        [jnp.where(is_q, r[:, h * LANES:(h + 1) * LANES] * q_scale, 0.0)
         for h in range(N_HEADS)], axis=0).astype(BF16)
    rqi = jnp.concatenate(
        [jnp.where(is_q, 0.0, r[:, h * LANES:(h + 1) * LANES]) for h in range(N_HEADS)],
        axis=0).astype(BF16)
    wt = wt_ref[0]
    t_idx = qb * Q_BLOCK + lax.broadcasted_iota(jnp.int32, (1, LANES), 1)
    row_iota = lax.broadcasted_iota(jnp.int32, (KEY_CHUNK, LANES), 0)
    nt = (((1,), (1,)), ((), ()))

    last = kk_ref.shape[1] - 1

    def keys_dot(c, w):
        return lax.dot_general(kk_ref[0, jnp.minimum(c, last)], w, nt, preferred_element_type=F32)


    def write_scores(c, lg_ref):
        sc = wt[0:1, :] * jnp.maximum(lg_ref[:, 0:LANES], 0.0)
        for h in range(1, N_HEADS):
            sc = sc + wt[h:h + 1, :] * jnp.maximum(lg_ref[:, h * LANES:(h + 1) * LANES], 0.0)
        s_idx = c * KEY_CHUNK + row_iota
        sc_ref[c] = jnp.where(s_idx <= t_idx, sc, -jnp.inf)

    def score_pair(j, carry):
        buf_b[...] = keys_dot(2 * j + 1, rqi)
        write_scores(2 * j, buf_a)
        buf_a[...] = keys_dot(2 * j + 2, rqi)
        write_scores(2 * j + 1, buf_b)
        return carry

    n_full = n_ch // 2
    odd = n_ch - 2 * n_full
    buf_a[...] = keys_dot(0, rqi)
    lax.fori_loop(0, n_full, score_pair, 0)

    @pl.when(odd == 1)
    def _():
        write_scores(n_ch - 1, buf_a)
        sc_ref[n_ch] = jnp.full((KEY_CHUNK, LANES), -jnp.inf, F32)

    k_row = jnp.minimum(t_idx + 1, TOPK).astype(F32)

    slabs = [slice(s * COUNT_ROWS, (s + 1) * COUNT_ROWS) for s in range(KEY_CHUNK // COUNT_ROWS)]

    def count_rows(pred):
        def body(j, acc):
            for half in range(2):
                for rows in slabs:
                    acc = acc + jnp.where(pred(sc_ref[2 * j + half, rows, :]), 1.0, 0.0)
            return acc
        acc = lax.fori_loop(0, n_pair, body, jnp.zeros((COUNT_ROWS, LANES), F32))
        return acc.sum(axis=0, keepdims=True)

    def bit_body(i, carry):
        u, cnt_u = carry
        trial = u | lax.shift_left(jnp.int32(1), jnp.asarray(31 - i, jnp.int32))
        cvec = _key_to_float(trial)
        cnt = count_rows(lambda x: x >= cvec)
        take = cnt >= k_row
        return jnp.where(take, trial, u), jnp.where(take, cnt, cnt_u)

    u, cnt_thr = lax.fori_loop(
        0, 32, bit_body, (jnp.zeros((1, LANES), jnp.int32), jnp.zeros((1, LANES), F32)))
    thr = _key_to_float(u)

    @pl.when(jnp.max(cnt_thr - k_row) > 0.0)
    def _():
        n_tie = count_rows(lambda x: x == thr)
        need = k_row - (cnt_thr - n_tie)
        ri = lax.broadcasted_iota(jnp.int32, (KEY_CHUNK, KEY_CHUNK), 0)
        ci = lax.broadcasted_iota(jnp.int32, (KEY_CHUNK, KEY_CHUNK), 1)
        before = jnp.where(ci < ri, 1.0, 0.0).astype(BF16)

        def drop_body(c, seen):
            x = sc_ref[c]
            tie = x == thr
            tie_f = jnp.where(tie, 1.0, 0.0)
            rank = seen + jnp.dot(before, tie_f.astype(BF16), preferred_element_type=F32)
            sc_ref[c] = jnp.where(tie, jnp.where(rank >= need, -jnp.inf, x), x)
            return seen + jnp.sum(tie_f, axis=0, keepdims=True)

        lax.fori_loop(0, n_ch, drop_body, jnp.zeros((1, LANES), F32))

    def attend(c, st_ref, carry):
        ms, accs = carry
        vt = vt_ref[0, c]
        sel = sc_ref[c] >= thr
        new_ms, new_accs = [], []
        for pr in range(N_HEADS // 2):
            ps, alphas = [], []
            for hh in range(2):
                h = 2 * pr + hh
                m_old = ms[h]
                s = jnp.where(sel, st_ref[:, h * LANES:(h + 1) * LANES], -jnp.inf)
                m_new = jnp.maximum(m_old, jnp.max(s, axis=0, keepdims=True))
                ps.append(jnp.exp2(s - m_new).astype(BF16))
                alphas.append(jnp.exp2(m_old - m_new))
                new_ms.append(m_new)
            pv = jnp.dot(vt, jnp.concatenate(ps, axis=1), preferred_element_type=F32)
            new_accs.append(jnp.concatenate(alphas, axis=1) * accs[pr] + pv)
        return tuple(new_ms), tuple(new_accs)

    def attn_pair(j, carry):
        buf_b[...] = keys_dot(2 * j + 1, rq)
        carry = attend(2 * j, buf_a, carry)
        buf_a[...] = keys_dot(2 * j + 2, rq)
        return attend(2 * j + 1, buf_b, carry)

    ms0 = tuple(jnp.full((1, LANES), -1e30, F32) for _ in range(N_HEADS))
    accs0 = tuple(jnp.zeros((V_ROWS, 2 * LANES), F32) for _ in range(N_HEADS // 2))
    buf_a[...] = keys_dot(0, rq)
    carry = lax.fori_loop(0, n_full, attn_pair, (ms0, accs0))
    _, accs = lax.cond(odd == 1, lambda cr: attend(n_ch - 1, buf_a, cr), lambda cr: cr, carry)
    outs = []
    for h in range(N_HEADS):
        a = accs[h // 2][:, (h % 2) * LANES:(h % 2 + 1) * LANES]
        outs.append(a[0:HEAD_DIM] * (1.0 / a[HEAD_DIM:HEAD_DIM + 1]))
    o_ref[0] = jnp.concatenate(outs, axis=0).T.astype(BF16)


def _dsa(r, kk, vt, wt):
    B, S, _ = r.shape
    n_kc = S // KEY_CHUNK
    return pl.pallas_call(
        _dsa_kernel,
        grid=(B, S // Q_BLOCK),
        in_specs=[
            pl.BlockSpec((1, Q_BLOCK, N_HEADS * LANES), lambda b, i: (b, i, 0)),
            pl.BlockSpec((1, n_kc, KEY_CHUNK, LANES), lambda b, i: (b, 0, 0, 0)),
            pl.BlockSpec((1, n_kc, V_ROWS, KEY_CHUNK), lambda b, i: (b, 0, 0, 0)),
            pl.BlockSpec((1, N_HEADS, Q_BLOCK), lambda b, i: (b, 0, i)),
        ],
        out_specs=pl.BlockSpec((1, Q_BLOCK, WIDTH_A), lambda b, i: (b, i, 0)),
        out_shape=jax.ShapeDtypeStruct((B, S, WIDTH_A), BF16),
        scratch_shapes=[pltpu.VMEM((n_kc + 1, KEY_CHUNK, LANES), F32),
                        pltpu.VMEM((KEY_CHUNK, N_HEADS * LANES), F32),
                        pltpu.VMEM((KEY_CHUNK, N_HEADS * LANES), F32)],
        compiler_params=pltpu.CompilerParams(
            dimension_semantics=("parallel", "parallel"), vmem_limit_bytes=VMEM_LIMIT),
        name="dsa_attention",
    )(r, kk, vt, wt)


def _out_ffn_kernel(x_ref, a_ref, b_ref, wo_ref, g_ref, w1_ref, w2_ref, o_ref):
    ab = jnp.concatenate([a_ref[...], b_ref[...]], axis=1)
    o_ref[...] = x_ref[...] + jnp.dot(ab, wo_ref[...], preferred_element_type=F32)
    o_ref[...] = _ffn(o_ref[...], g_ref[...], w1_ref, w2_ref)


def _out_ffn(x, a, b, wo, g, w1, w2, tm):
    T, D = x.shape
    const = lambda *shape: pl.BlockSpec(shape, lambda i: (0,) * len(shape),
                                        pipeline_mode=pl.Buffered(1))
    return pl.pallas_call(
        _out_ffn_kernel,
        grid=(T // tm,),
        in_specs=[
            pl.BlockSpec((tm, D), lambda i: (i, 0)),
            pl.BlockSpec((tm, WIDTH_A), lambda i: (i, 0)),
            pl.BlockSpec((tm, WIDTH_B), lambda i: (i, 0)),
            const(WIDTH_A + WIDTH_B, D),
            const(1, D),
            const(D, D_FF),
            const(D_FF, D),
        ],
        out_specs=pl.BlockSpec((tm, D), lambda i: (i, 0)),
        out_shape=jax.ShapeDtypeStruct((T, D), F32),
        compiler_params=pltpu.CompilerParams(
            dimension_semantics=("parallel",), vmem_limit_bytes=VMEM_LIMIT),
        name="out_proj_ffn",
    )(x, a, b, wo, g, w1, w2)


def _conv_ffn_kernel(x_ref, gn_ref, wp1_ref, wdw_ref, bdw_ref, gc_ref, wp2_ref,
                     g_ref, w1_ref, w2_ref, o_ref, ybuf):
    tm = x_ref.shape[1]
    x = x_ref[0]
    h = _rms_rows(x, gn_ref[...]).astype(BF16)
    z = jnp.dot(h, wp1_ref[...], preferred_element_type=F32)
    y = z[:, :D_MODEL] * _sigmoid(z[:, D_MODEL:])

    @pl.when(pl.program_id(1) == 0)
    def _():
        ybuf[0:CONV_HALO, :] = jnp.zeros((CONV_HALO, D_MODEL), F32)
        ybuf[CONV_HALO + tm:, :] = jnp.zeros((SUBLANES, D_MODEL), F32)

    ybuf[CONV_HALO:CONV_HALO + tm, :] = y
    first = CONV_HALO - (CONV_WIDTH - 1)
    span = tm + SUBLANES
    acc = None
    for res in reversed(range(SUBLANES)):
        part = None
        for j in range(CONV_WIDTH):
            if (first + j) % SUBLANES == res:
                base = first + j - res
                term = wdw_ref[j:j + 1, :] * ybuf[base:base + span, :]
                part = term if part is None else part + term
        acc = part if acc is None else part + pltpu.roll(acc, span - 1, axis=0)
    acc = acc[0:tm, :] + bdw_ref[...]
    ybuf[0:CONV_HALO, :] = ybuf[tm:tm + CONV_HALO, :]

    c = _rms_rows(acc, gc_ref[...])
    c = (c * _sigmoid(c)).astype(BF16)
    o_ref[0] = x + jnp.dot(c, wp2_ref[...], preferred_element_type=F32)
    o_ref[0] = _ffn(o_ref[0], g_ref[...], w1_ref, w2_ref)


def _conv_ffn(x, gn, wp1, wdw, bdw, gc, wp2, g, w1, w2, tm):
    B, S, D = x.shape
    const = lambda *shape: pl.BlockSpec(shape, lambda b, i: (0,) * len(shape),
                                        pipeline_mode=pl.Buffered(1))
    return pl.pallas_call(
        _conv_ffn_kernel,
        grid=(B, S // tm),
        in_specs=[
            pl.BlockSpec((1, tm, D), lambda b, i: (b, i, 0)),
            const(1, D),
            const(D, 2 * D),
            const(CONV_WIDTH, D),
            const(1, D),
            const(1, D),
            const(D, D),
            const(1, D),
            const(D, D_FF),
            const(D_FF, D),
        ],
        out_specs=pl.BlockSpec((1, tm, D), lambda b, i: (b, i, 0)),
        out_shape=jax.ShapeDtypeStruct((B, S, D), F32),
        scratch_shapes=[pltpu.VMEM((tm + CONV_HALO + SUBLANES, D), F32)],
        compiler_params=pltpu.CompilerParams(
            dimension_semantics=("arbitrary", "arbitrary"), vmem_limit_bytes=VMEM_LIMIT),
        name="conv_module_ffn",
    )(x, gn, wp1, wdw, bdw, gc, wp2, g, w1, w2)


def _column_plan():
    q0, k0, v0 = 0, WIDTH_A, WIDTH_A + HEAD_DIM
    qi0 = v0 + HEAD_DIM
    ki0 = qi0 + N_HEADS * IDX_DIM
    wi0 = ki0 + IDX_DIM
    u0 = wi0 + N_HEADS
    vb0 = u0 + WIDTH_B
    main = -np.ones((MAIN_COLS,), np.int64)
    swap = -np.ones((ROPE_COLS,), np.int64)

    def put(dst, src, width):
        half = width // 2
        for d in range(width):
            main[dst + d] = src + d
            swap[dst + d] = src + (d + half) % width

    for h in range(N_HEADS):
        put(h * LANES, q0 + h * HEAD_DIM, HEAD_DIM)
        put(h * LANES + HEAD_DIM, qi0 + h * IDX_DIM, IDX_DIM)
    put(N_HEADS * LANES, k0, HEAD_DIM)
    put(N_HEADS * LANES + HEAD_DIM, ki0, IDX_DIM)
    main[ROPE_COLS:ROPE_COLS + HEAD_DIM] = v0 + np.arange(HEAD_DIM)
    main[ROPE_COLS + HEAD_DIM:ROPE_COLS + HEAD_DIM + N_HEADS] = wi0 + np.arange(N_HEADS)
    main[ROPE_COLS + LANES:ROPE_COLS + LANES + WIDTH_B] = u0 + np.arange(WIDTH_B)
    main[ROPE_COLS + LANES + WIDTH_B:] = vb0 + np.arange(WIDTH_B)
    return main, swap


def _take_cols(w, idx):
    wz = jnp.concatenate([w, jnp.zeros((w.shape[0], 1), w.dtype)], axis=1)
    return jnp.take(wz, jnp.asarray(np.where(idx < 0, w.shape[1], idx)), axis=1)


def _rope_lane_tables(g_q, g_k, g_kidx):
    half_q, half_i = HEAD_DIM // 2, IDX_DIM // 2
    f_q = ROPE_THETA ** (-jnp.arange(half_q, dtype=F32) * 2.0 / HEAD_DIM)
    f_i = ROPE_THETA ** (-jnp.arange(half_i, dtype=F32) * 2.0 / IDX_DIM)
    pad = jnp.zeros((LANES - HEAD_DIM - IDX_DIM,), F32)
    freq = jnp.concatenate([f_q, f_q, f_i, f_i, pad])
    sgn = jnp.concatenate([-jnp.ones((half_q,), F32), jnp.ones((half_q,), F32),
                           -jnp.ones((half_i,), F32), jnp.ones((half_i,), F32), pad])
    swap_q = jnp.concatenate([g_q[half_q:], g_q[:half_q]])
    swap_k = jnp.concatenate([g_k[half_q:], g_k[:half_q]])
    swap_i = jnp.concatenate([g_kidx[half_i:], g_kidx[:half_i]])
    ones_i = jnp.ones((IDX_DIM,), F32)
    q_main = jnp.concatenate([g_q, ones_i, pad])
    q_swap = jnp.concatenate([swap_q, ones_i, pad])
    k_main = jnp.concatenate([g_k, g_kidx, pad])
    k_swap = jnp.concatenate([swap_k, swap_i, pad])
    gm = jnp.concatenate([jnp.tile(q_main, N_HEADS), k_main])
    gs = jnp.concatenate([jnp.tile(q_swap, N_HEADS), k_swap])
    return freq[None], sgn[None], gm[None], gs[None]


def kernel(x, positions, ev_g_norm, ev_w_in, ev_g_q, ev_g_k, ev_g_kidx, ev_g_vb, ev_w_s, ev_b_s,
           ev_w_out, od_g_norm, od_w_pw1, od_w_dw, od_b_dw, od_g_conv, od_w_pw2, ff_g_norm,
           ff_w1, ff_w2):
    B, S, D = x.shape
    assert D == D_MODEL and S % (2 * KEY_CHUNK) == 0
    tm = 512
    tm_mlp = 1024

    main_idx, swap_idx = _column_plan()
    w_in = ev_w_in[0]
    wm = _take_cols(w_in, main_idx).astype(BF16)
    ws = _take_cols(w_in, swap_idx).astype(BF16)
    freq, sgn, gm, gs = _rope_lane_tables(ev_g_q[0], ev_g_k[0], ev_g_kidx[0])
    vws = jnp.concatenate([jnp.ones((HEAD_DIM,), F32), jnp.full((N_HEADS,), IDX_SCALE, F32),
                           jnp.zeros((LANES - HEAD_DIM - N_HEADS,), F32)])[None]
    bsp = jnp.repeat(ev_b_s[0].T, GROUP_DIM, axis=1)
    pos = positions.astype(F32)[..., None]

    r, kk, vt, wt, b_out = _in_stage(
        x, pos, ev_g_norm[0][None], wm, ws, freq, sgn, gm, gs, vws, ev_g_vb[0][None],
        ev_w_s[0], bsp, tm)
    a_out = _dsa(r, kk.reshape(B, S // KEY_CHUNK, KEY_CHUNK, LANES), vt, wt)
    x1 = _out_ffn(x.reshape(B * S, D), a_out.reshape(B * S, WIDTH_A), b_out.reshape(B * S, WIDTH_B),
                  ev_w_out[0].astype(BF16), ff_g_norm[0][None], ff_w1[0].astype(BF16),
                  ff_w2[0].astype(BF16), tm_mlp)

    x2 = _conv_ffn(x1.reshape(B, S, D), od_g_norm[0][None], od_w_pw1[0].astype(BF16), od_w_dw[0],
                   od_b_dw[0][None], od_g_conv[0][None], od_w_pw2[0].astype(BF16),
                   ff_g_norm[1][None], ff_w1[1].astype(BF16), ff_w2[1].astype(BF16), tm_mlp)
    return x2
```
